```python
import math
import jax, jax.numpy as jnp
from jax import lax
import numpy as np

D_MODEL = 2048
BATCH = 2
SEQ = 8192
DEPTH = 1

F_WIDTH = D_MODEL // 2
F_GROUPS = 8
F_GROUP_DIM = F_WIDTH // F_GROUPS
SSD_WIDTH = D_MODEL
HEAD_DIM = 64
SSD_HEADS = SSD_WIDTH // HEAD_DIM
N_BC_GROUPS = 4
D_STATE = 128
D_CONV = 5
CONV_PAD = D_CONV // 2
CONV_CH = SSD_WIDTH + 2 * N_BC_GROUPS * D_STATE
CHUNK = 128
IN_SIZES = [F_WIDTH, F_WIDTH, SSD_WIDTH, CONV_CH, SSD_HEADS, SSD_HEADS, 2 * D_MODEL]
IN_SPLITS = [int(v) for v in np.cumsum(IN_SIZES)[:-1]]
N_IN = int(sum(IN_SIZES))
NORM_EPS = 1e-5
DEEPNORM_ALPHA = (2.0 * DEPTH) ** 0.25
DEEPNORM_BETA = (8.0 * DEPTH) ** -0.25

kernel_name = "hybrid_fnet_ssd_gated_deepnorm"


def layer_norm(x, g, b):
    xf = x.astype(jnp.float32)
    mu = jnp.mean(xf, axis=-1, keepdims=True)
    var = jnp.mean(jnp.square(xf - mu), axis=-1, keepdims=True)
    y = (xf - mu) * lax.rsqrt(var + NORM_EPS)
    return (y * g.astype(jnp.float32) + b.astype(jnp.float32)).astype(x.dtype)


def gated_rms_norm(y, z, w):
    b, l, d = y.shape
    h = (y * jax.nn.silu(z)).astype(jnp.float32).reshape(b, l, N_BC_GROUPS, d // N_BC_GROUPS)
    h = h * lax.rsqrt(jnp.mean(jnp.square(h), axis=-1, keepdims=True) + NORM_EPS)
    return (h.reshape(b, l, d) * w.astype(jnp.float32)).astype(y.dtype)


def centred_depthwise_conv(u, w, bias):
    c = u.shape[-1]
    out = lax.conv_general_dilated(
        u, w[:, None, :].astype(u.dtype), window_strides=(1,), padding=[(CONV_PAD, CONV_PAD)],
        dimension_numbers=("NWC", "WIO", "NWC"), feature_group_count=c)
    return out + bias


def ssd_scan(xs, dt, A, Bm, Cm):
    b, l, h, p = xs.shape
    g, n = Bm.shape[2], Bm.shape[3]
    r = h // g
    c = l // CHUNK
    xdt = (xs * dt[..., None]).reshape(b, c, CHUNK, g, r, p)
    Bc = Bm.reshape(b, c, CHUNK, g, n)
    Cc = Cm.reshape(b, c, CHUNK, g, n)
    a = (dt * A).reshape(b, c, CHUNK, g, r)
    a_cs = jnp.cumsum(a, axis=2).transpose(0, 1, 3, 4, 2)
    seg = a_cs[..., :, None] - a_cs[..., None, :]
    lower = jnp.tril(jnp.ones((CHUNK, CHUNK), dtype=bool))
    L = jnp.exp(jnp.where(lower, seg, -jnp.inf))
    cb = jnp.einsum('bclgn,bcsgn->bcgls', Cc, Bc)
    y_diag = jnp.einsum('bcgrls,bcsgrp->bclgrp', cb[:, :, :, None] * L, xdt)
    decay_states = jnp.exp(a_cs[..., -1:] - a_cs)
    states = jnp.einsum('bcsgn,bcgrs,bcsgrp->bcgrpn', Bc, decay_states, xdt)
    chunk_decay = jnp.exp(a_cs[..., -1])

    def step(carry, inp):
        st, dec = inp
        return carry * dec[..., None, None] + st, carry

    _, prev = lax.scan(step, jnp.zeros_like(states[:, 0]),
                       (jnp.moveaxis(states, 1, 0), jnp.moveaxis(chunk_decay, 1, 0)))
    prev = jnp.moveaxis(prev, 0, 1)
    y_off = jnp.einsum('bclgn,bcgrpn,bcgrl->bclgrp', Cc, prev, jnp.exp(a_cs))
    return (y_diag + y_off).reshape(b, l, h, p)


def hybrid_layer(x, w_in, b_gate, conv_w, conv_b, dt_bias_fwd, a_log_fwd, dt_bias_bwd, a_log_bwd,
                 d_skip, ssd_norm_w, f_mix_w, f_mix_b, w_branch_f, w_branch_s, w_out, ln_g, ln_b):
    b, l, _ = x.shape
    proj = jnp.einsum('bld,dn->bln', x, w_in)
    u_f, z_f, z_s, xbc, dt_f, dt_b, gates = jnp.split(proj, IN_SPLITS, axis=-1)
    g_f, g_s = jnp.split(jax.nn.sigmoid(gates + b_gate), 2, axis=-1)

    uf = u_f.reshape(b, l, F_GROUPS, F_GROUP_DIM).astype(jnp.float32)
    four = jnp.fft.fft2(uf, axes=(1, 3), norm="ortho").real.astype(x.dtype)
    mixed = jnp.einsum('blgi,gio->blgo', four, f_mix_w).reshape(b, l, F_WIDTH) + f_mix_b
    p_f = jnp.einsum('blf,fd->bld', mixed * jax.nn.silu(z_f), w_branch_f)

    xbc = jax.nn.silu(centred_depthwise_conv(xbc, conv_w, conv_b))
    xs, Bm, Cm = jnp.split(xbc, [SSD_WIDTH, SSD_WIDTH + N_BC_GROUPS * D_STATE], axis=-1)
    xs = xs.reshape(b, l, SSD_HEADS, HEAD_DIM)
    Bm = Bm.reshape(b, l, N_BC_GROUPS, D_STATE)
    Cm = Cm.reshape(b, l, N_BC_GROUPS, D_STATE)
    dtf = jax.nn.softplus(dt_f.astype(jnp.float32) + dt_bias_fwd.astype(jnp.float32))
    dtb = jax.nn.softplus(dt_b.astype(jnp.float32) + dt_bias_bwd.astype(jnp.float32))
    A_f = -jnp.exp(a_log_fwd.astype(jnp.float32))
    A_b = -jnp.exp(a_log_bwd.astype(jnp.float32))
    y_fwd = ssd_scan(xs, dtf, A_f, Bm, Cm)
    flip = lambda t: jnp.flip(t, axis=1)
    y_bwd = flip(ssd_scan(flip(xs), flip(dtb), A_b, flip(Bm), flip(Cm)))
    y = (y_fwd + y_bwd + xs * d_skip[:, None]).reshape(b, l, SSD_WIDTH).astype(x.dtype)
    y = gated_rms_norm(y, z_s, ssd_norm_w)
    p_s = jnp.einsum('bls,sd->bld', y, w_branch_s)

    out = jnp.einsum('bld,de->ble', g_f * p_f + g_s * p_s, w_out)
    return layer_norm(DEEPNORM_ALPHA * x + out, ln_g, ln_b)


def setup_inputs(seed: int = 0) -> dict:
    key = jax.random.key(seed)
    ks = jax.random.split(key, 18)
    f32 = jnp.float32
    nrm = lambda k, s, sc: jax.random.normal(k, s, f32) * sc

    def dt_bias(k):
        u = jax.random.uniform(k, (DEPTH, SSD_HEADS), f32)
        dt = jnp.exp(u * (math.log(0.1) - math.log(0.001)) + math.log(0.001))
        return dt + jnp.log(-jnp.expm1(-dt))

    def a_log(k):
        return jnp.log(jax.random.uniform(k, (DEPTH, SSD_HEADS), f32, 1.0, 16.0))

    return {
        "x": nrm(ks[0], (BATCH, SEQ, D_MODEL), 1.0),
        "w_in": nrm(ks[1], (DEPTH, D_MODEL, N_IN), D_MODEL ** -0.5),
        "b_gate": nrm(ks[2], (DEPTH, 2 * D_MODEL), 0.01),
        "conv_w": nrm(ks[3], (DEPTH, D_CONV, CONV_CH), D_CONV ** -0.5),
        "conv_b": nrm(ks[4], (DEPTH, CONV_CH), 0.01),
        "dt_bias_fwd": dt_bias(ks[5]),
        "a_log_fwd": a_log(ks[6]),
        "dt_bias_bwd": dt_bias(ks[7]),
        "a_log_bwd": a_log(ks[8]),
        "d_skip": 1.0 + nrm(ks[9], (DEPTH, SSD_HEADS), 0.01),
        "ssd_norm_w": 1.0 + nrm(ks[10], (DEPTH, SSD_WIDTH), 0.01),
        "f_mix_w": nrm(ks[11], (DEPTH, F_GROUPS, F_GROUP_DIM, F_GROUP_DIM), F_GROUP_DIM ** -0.5),
        "f_mix_b": nrm(ks[12], (DEPTH, F_WIDTH), 0.01),
        "w_branch_f": nrm(ks[13], (DEPTH, F_WIDTH, D_MODEL), F_WIDTH ** -0.5 * DEEPNORM_BETA),
        "w_branch_s": nrm(ks[14], (DEPTH, SSD_WIDTH, D_MODEL), SSD_WIDTH ** -0.5 * DEEPNORM_BETA),
        "w_out": nrm(ks[15], (DEPTH, D_MODEL, D_MODEL), D_MODEL ** -0.5 * DEEPNORM_BETA),
        "ln_g": 1.0 + nrm(ks[16], (DEPTH, D_MODEL), 0.01),
        "ln_b": nrm(ks[17], (DEPTH, D_MODEL), 0.01),
    }


def reference(x, w_in, b_gate, conv_w, conv_b, dt_bias_fwd, a_log_fwd, dt_bias_bwd, a_log_bwd,
              d_skip, ssd_norm_w, f_mix_w, f_mix_b, w_branch_f, w_branch_s, w_out, ln_g, ln_b):
    for i in range(DEPTH):
        x = hybrid_layer(x, w_in[i], b_gate[i], conv_w[i], conv_b[i], dt_bias_fwd[i], a_log_fwd[i],
                         dt_bias_bwd[i], a_log_bwd[i], d_skip[i], ssd_norm_w[i], f_mix_w[i], f_mix_b[i],
                         w_branch_f[i], w_branch_s[i], w_out[i], ln_g[i], ln_b[i])
    return x
```

```python
import functools
import math

import numpy as np
import jax
import jax.numpy as jnp
from jax import lax
from jax.experimental import pallas as pl
from jax.experimental.pallas import tpu as pltpu

F32 = jnp.float32
BF16 = jnp.bfloat16

F_GROUPS = 8
HEAD_DIM = 64
N_BC_GROUPS = 4
D_STATE = 128
D_CONV = 5
CONV_PAD = D_CONV // 2
CHUNK = 128
NORM_EPS = 1e-5

LANES = 128
SUBLANES = 8
BF16_ROWS = 16
VMEM_LIMIT_BYTES = 56 * 1024 * 1024

HIGHEST = lax.Precision.HIGHEST


def _sigmoid(v):
    return 1.0 / (1.0 + jnp.exp(-v))


def _softplus(v):
    return jnp.maximum(v, 0.0) + jnp.log1p(jnp.exp(-jnp.abs(v)))


def _dot(a, b, precision=None):
    return jnp.dot(a, b, preferred_element_type=F32, precision=precision)


def _inproj_kernel(x_ref, w_ref, b_ref, wdt_ref, bdt_ref, o_ref, dt_ref, dtt_ref, xb_ref,
                   *, silu_tiles, sigmoid_tiles):
    j = pl.program_id(1)

    @pl.when(j == 0)
    def _():
        xb = x_ref[...].astype(BF16)
        xb_ref[...] = xb
        dt = _softplus(_dot(xb, wdt_ref[...]) + bdt_ref[...])
        dt_ref[...] = dt
        dtt_ref[...] = dt.T

    acc = _dot(xb_ref[...], w_ref[...]) + b_ref[...]

    def tile_in(ranges):
        m = None
        for lo, hi in ranges:
            c = (j >= lo) & (j < hi)
            m = c if m is None else (m | c)
        return m

    is_silu = tile_in(silu_tiles)
    is_sig = tile_in(sigmoid_tiles)

    @pl.when(is_silu)
    def _():
        o_ref[...] = (acc * _sigmoid(acc)).astype(o_ref.dtype)

    @pl.when(is_sig)
    def _():
        o_ref[...] = _sigmoid(acc).astype(o_ref.dtype)

    @pl.when(jnp.logical_not(is_silu | is_sig))
    def _():
        o_ref[...] = acc.astype(o_ref.dtype)


def _inproj(x2, w_main, b_main, w_dt, b_dt, *, tm, tn, silu_tiles, sigmoid_tiles):
    t, d = x2.shape
    nm = w_main.shape[1]
    kern = functools.partial(_inproj_kernel, silu_tiles=silu_tiles, sigmoid_tiles=sigmoid_tiles)
    return pl.pallas_call(
        kern,
        grid=(t // tm, nm // tn),
        in_specs=[
            pl.BlockSpec((tm, d), lambda i, j: (i, 0)),
            pl.BlockSpec((d, tn), lambda i, j: (0, j)),
            pl.BlockSpec((1, tn), lambda i, j: (0, j)),
            pl.BlockSpec((d, LANES), lambda i, j: (0, 0)),
            pl.BlockSpec((1, LANES), lambda i, j: (0, 0)),
        ],
        out_specs=[
            pl.BlockSpec((tm, tn), lambda i, j: (i, j)),
            pl.BlockSpec((tm, LANES), lambda i, j: (i, 0)),
            pl.BlockSpec((LANES, tm), lambda i, j: (0, i)),
        ],
        out_shape=[
            jax.ShapeDtypeStruct((t, nm), BF16),
            jax.ShapeDtypeStruct((t, LANES), F32),
            jax.ShapeDtypeStruct((LANES, t), F32),
        ],
        scratch_shapes=[pltpu.VMEM((tm, d), BF16)],
        compiler_params=pltpu.CompilerParams(
            dimension_semantics=("arbitrary", "arbitrary"),
            vmem_limit_bytes=VMEM_LIMIT_BYTES),
        name="inproj",
    )(x2, w_main, b_main, w_dt, b_dt)


def _fnet_kernel(u_ref, zf_ref, cs_ref, d1_ref, twr_ref, twi_ref, d2_ref, wmix_ref, bmix_ref,
                 o_ref, vr_ref, vi_ref, zr_ref, zi_ref, x_ref, *, n_min, p1, p2, row_blk, scale):
    seq = u_ref.shape[0]
    n_maj = seq // n_min

    def s0(n_hi, c):
        r0 = pl.multiple_of(n_hi * n_maj, n_maj)
        v = _dot(u_ref[pl.ds(r0, n_maj), :], cs_ref[...])
        d0 = pl.multiple_of(n_hi * p1, SUBLANES)
        vr_ref[pl.ds(d0, n_maj), :] = v[:, :LANES]
        vi_ref[pl.ds(d0, n_maj), :] = v[:, LANES:]
        return c
    lax.fori_loop(0, n_min, s0, 0)

    def s1(n_lo, c):
        vr = vr_ref[pl.ds(n_lo, n_min, stride=p1), :]
        vi = vi_ref[pl.ds(n_lo, n_min, stride=p1), :]
        r = jnp.concatenate([vr, vi], axis=0).astype(BF16)
        z = _dot(d1_ref[...], r)
        zr, zi = z[:n_min], z[n_min:]
        tr, ti = twr_ref[n_lo], twi_ref[n_lo]
        d0 = pl.multiple_of(n_lo * p2, SUBLANES)
        zr_ref[pl.ds(d0, n_min), :] = zr * tr - zi * ti
        zi_ref[pl.ds(d0, n_min), :] = zr * ti + zi * tr
        return c
    lax.fori_loop(0, n_maj, s1, 0)

    def s2(k_min, c):
        zr = zr_ref[pl.ds(k_min, n_maj, stride=p2), :]
        zi = zi_ref[pl.ds(k_min, n_maj, stride=p2), :]
        r = jnp.concatenate([zr, zi], axis=0).astype(BF16)
        x_ref[pl.ds(k_min, n_maj, stride=n_min), :] = _dot(d2_ref[...], r)
        return c
    lax.fori_loop(0, n_min, s2, 0)

    def s3(i, c):
        r0 = pl.multiple_of(i * row_blk, row_blk)
        four = (x_ref[pl.ds(r0, row_blk), :] * scale).astype(BF16)
        mixed = _dot(four, wmix_ref[...]) + bmix_ref[...]
        o_ref[pl.ds(r0, row_blk), :] = (mixed * zf_ref[pl.ds(r0, row_blk), :].astype(F32)
                                        ).astype(o_ref.dtype)
        return c
    lax.fori_loop(0, seq // row_blk, s3, 0)


def _dft_mats(n):
    k = np.arange(n)
    ang = 2.0 * np.pi * ((k[:, None] * k[None, :]) % n) / n
    return np.cos(ang), np.sin(ang)


def _fnet(proj, f_mix_w, f_mix_b, *, seq, uf_blk0, zf_blk0):
    b = proj.shape[0]
    gd = LANES
    n_maj = LANES
    n_min = seq // n_maj
    p1 = n_maj + SUBLANES
    p2 = n_min + SUBLANES
    c128, s128 = _dft_mats(gd)
    cs = jnp.asarray(np.concatenate([c128, -s128], axis=1), BF16)
    cm, sm = _dft_mats(n_min)
    d1 = jnp.asarray(np.block([[cm, sm], [-sm, cm]]), BF16)
    d2 = jnp.asarray(np.concatenate([c128, s128], axis=1), BF16)
    nl = np.arange(n_maj)[:, None]
    km = np.arange(n_min)[None, :]
    ang = 2.0 * np.pi * ((nl * km) % seq) / seq
    twr = jnp.broadcast_to(jnp.asarray(np.cos(ang), F32)[:, :, None], (n_maj, n_min, LANES))
    twi = jnp.broadcast_to(jnp.asarray(-np.sin(ang), F32)[:, :, None], (n_maj, n_min, LANES))
    wmix = f_mix_w.astype(BF16)
    bmix = f_mix_b.reshape(F_GROUPS, 1, gd).astype(F32)
    scale = 1.0 / math.sqrt(seq * gd)
    row_blk = min(seq, 512)
    kern = functools.partial(_fnet_kernel, n_min=n_min, p1=p1, p2=p2, row_blk=row_blk, scale=scale)
    const2 = lambda shape: pl.BlockSpec(shape, lambda bi, g: (0, 0))
    return pl.pallas_call(
        kern,
        grid=(b, F_GROUPS),
        in_specs=[
            pl.BlockSpec((None, seq, gd), lambda bi, g: (bi, 0, uf_blk0 + g)),
            pl.BlockSpec((None, seq, gd), lambda bi, g: (bi, 0, zf_blk0 + g)),
            const2((gd, 2 * gd)),
            const2((2 * n_min, 2 * n_min)),
            pl.BlockSpec((n_maj, n_min, LANES), lambda bi, g: (0, 0, 0)),
            pl.BlockSpec((n_maj, n_min, LANES), lambda bi, g: (0, 0, 0)),
            const2((n_maj, 2 * n_maj)),
            pl.BlockSpec((None, gd, gd), lambda bi, g: (g, 0, 0)),
            pl.BlockSpec((None, 1, gd), lambda bi, g: (g, 0, 0)),
        ],
        out_specs=pl.BlockSpec((None, seq, gd), lambda bi, g: (bi, 0, g)),
        out_shape=jax.ShapeDtypeStruct((b, seq, F_GROUPS * gd), BF16),
        scratch_shapes=[
            pltpu.VMEM((n_min * p1, LANES), F32),
            pltpu.VMEM((n_min * p1, LANES), F32),
            pltpu.VMEM((n_maj * p2, LANES), F32),
            pltpu.VMEM((n_maj * p2, LANES), F32),
            pltpu.VMEM((seq, LANES), F32),
        ],
        compiler_params=pltpu.CompilerParams(
            dimension_semantics=("arbitrary", "arbitrary"),
            vmem_limit_bytes=VMEM_LIMIT_BYTES),
        name="fnet",
    )(proj, proj, cs, d1, twr, twi, d2, wmix, bmix)


def _conv_kernel(x_ref, p_ref, n_ref, w_ref, b_ref, o_ref, ext_ref, *, n_row_blocks):
    i = pl.program_id(1)
    tr = x_ref.shape[0]
    halo = SUBLANES
    prev = p_ref[...].astype(F32)[BF16_ROWS - halo:, :]
    nxt = n_ref[...].astype(F32)[:halo, :]
    ext_ref[pl.ds(0, halo), :] = jnp.where(i == 0, 0.0, prev)
    ext_ref[pl.ds(halo, tr), :] = x_ref[...].astype(F32)
    ext_ref[pl.ds(halo + tr, halo), :] = jnp.where(i == n_row_blocks - 1, 0.0, nxt)
    acc = jnp.zeros(o_ref.shape, F32) + b_ref[...]
    for k in range(D_CONV):
        acc = acc + ext_ref[pl.ds(halo - CONV_PAD + k, tr), :] * w_ref[pl.ds(k, 1), :]
    o_ref[...] = (acc * _sigmoid(acc)).astype(o_ref.dtype)


def _conv(proj, conv_w, conv_b, *, seq, tr, tc, xbc_blk0):
    b = proj.shape[0]
    cch = conv_w.shape[1]
    nrb = seq // tr
    hb = tr // BF16_ROWS
    last_hb = seq // BF16_ROWS - 1
    kern = functools.partial(_conv_kernel, n_row_blocks=nrb)
    return pl.pallas_call(
        kern,
        grid=(b, nrb, cch // tc),
        in_specs=[
            pl.BlockSpec((None, tr, tc), lambda bi, i, c: (bi, i, xbc_blk0 + c)),
            pl.BlockSpec((None, BF16_ROWS, tc),
                         lambda bi, i, c: (bi, jnp.maximum(i * hb - 1, 0), xbc_blk0 + c)),
            pl.BlockSpec((None, BF16_ROWS, tc),
                         lambda bi, i, c: (bi, jnp.minimum((i + 1) * hb, last_hb), xbc_blk0 + c)),
            pl.BlockSpec((D_CONV, tc), lambda bi, i, c: (0, c)),
            pl.BlockSpec((1, tc), lambda bi, i, c: (0, c)),
        ],
        out_specs=pl.BlockSpec((None, tr, tc), lambda bi, i, c: (bi, i, c)),
        out_shape=jax.ShapeDtypeStruct((b, seq, cch), BF16),
        scratch_shapes=[pltpu.VMEM((tr + 2 * SUBLANES, tc), F32)],
        compiler_params=pltpu.CompilerParams(
            dimension_semantics=("arbitrary", "arbitrary", "arbitrary"),
            vmem_limit_bytes=VMEM_LIMIT_BYTES),
        name="conv",
    )(proj, proj, proj, conv_w, conv_b)


def _ssd_direction(xs_ref, b_ref, c_ref, dt_ref, dtt_ref, alog_ref, alogc_ref, e_ref, st_ref, y_ref,
                   *, reverse, heads, col0):
    lc = CHUNK
    hpg = heads // N_BC_GROUPS
    gw = hpg * HEAD_DIM
    row = lax.broadcasted_iota(jnp.int32, (lc, lc), 0)
    col = lax.broadcasted_iota(jnp.int32, (lc, lc), 1)
    incl = (col >= row) if reverse else (col <= row)
    cum = incl.astype(F32)
    cum_t = ((row >= col) if reverse else (row <= col)).astype(F32)

    a_neg = -jnp.exp(alog_ref[...])
    a_neg_c = -jnp.exp(alogc_ref[...])
    dt = dt_ref[:, col0:col0 + heads]
    dtt = dtt_ref[col0:col0 + heads, :]
    a = dt * a_neg
    acs = _dot(cum, a, HIGHEST)
    acs_t = _dot(dtt * a_neg_c, cum_t, HIGHEST)
    last = acs[0:1, :] if reverse else acs[lc - 1:lc, :]
    stack = jnp.concatenate([
        dt,
        dt * jnp.exp(last - acs),
        jnp.exp(acs),
        jnp.broadcast_to(jnp.exp(last), (SUBLANES, heads)),
    ], axis=0)
    ex = _dot(stack, e_ref[...], HIGHEST)
    dt_e, dtdec_e, eacs_e = ex[:lc], ex[lc:2 * lc], ex[2 * lc:3 * lc]
    cdec_e = ex[3 * lc:3 * lc + 1]

    xs = xs_ref[...].astype(F32)
    xdt = xs * dt_e
    xdd = (xs * dtdec_e).astype(BF16)
    lane = lax.broadcasted_iota(jnp.int32, (lc, LANES), 1)
    low_half = lane < HEAD_DIM

    for g in range(N_BC_GROUPS):
        bg = b_ref[:, g * D_STATE:(g + 1) * D_STATE]
        cg = c_ref[:, g * D_STATE:(g + 1) * D_STATE]
        cb = lax.dot_general(cg, bg, (((1,), (1,)), ((), ())), preferred_element_type=F32)
        prev = st_ref[g]
        y_off = _dot(cg, prev.astype(BF16)) * eacs_e[:, g * gw:(g + 1) * gw]
        st = lax.dot_general(bg, xdd[:, g * gw:(g + 1) * gw], (((0,), (0,)), ((), ())),
                             preferred_element_type=F32)
        st_ref[g] = prev * cdec_e[:, g * gw:(g + 1) * gw] + st
        for hp in range(hpg // 2):
            ms = []
            for hh in range(2):
                h = g * hpg + 2 * hp + hh
                seg = acs[:, h:h + 1] - acs_t[h:h + 1, :]
                ms.append((cb * jnp.where(incl, jnp.exp(seg), 0.0)).astype(BF16))
            lhs = jnp.concatenate(ms, axis=1)
            c0 = g * gw + hp * LANES
            xpair = xdt[:, c0:c0 + LANES]
            rhs = jnp.concatenate([jnp.where(low_half, xpair, 0.0),
                                   jnp.where(low_half, 0.0, xpair)], axis=0).astype(BF16)
            y_ref[:, c0:c0 + LANES] = (_dot(lhs, rhs) + y_off[:, hp * LANES:(hp + 1) * LANES]
                                       ).astype(y_ref.dtype)


def _ssd_kernel(xsf_ref, bf_ref, cf_ref, dtf_ref, dttf_ref,
                xsb_ref, bb_ref, cb_ref, dtb_ref, dttb_ref,
                alog_ref, alogc_ref, e_ref, yf_ref, yb_ref, stf_ref, stb_ref, *, heads):
    @pl.when(pl.program_id(1) == 0)
    def _():
        stf_ref[...] = jnp.zeros_like(stf_ref)
        stb_ref[...] = jnp.zeros_like(stb_ref)

    _ssd_direction(xsf_ref, bf_ref, cf_ref, dtf_ref, dttf_ref, alog_ref.at[0], alogc_ref.at[0],
                   e_ref, stf_ref, yf_ref, reverse=False, heads=heads, col0=0)
    _ssd_direction(xsb_ref, bb_ref, cb_ref, dtb_ref, dttb_ref, alog_ref.at[1], alogc_ref.at[1],
                   e_ref, stb_ref, yb_ref, reverse=True, heads=heads, col0=heads)


def _ssd(xbc, dt, dtt, a_log, *, seq, heads):
    b = xbc.shape[0]
    nc = seq // CHUNK
    width = heads * HEAD_DIM
    bc_w = N_BC_GROUPS * D_STATE
    b_blk = width // bc_w
    gw = width // N_BC_GROUPS
    expand = jnp.asarray(np.kron(np.eye(heads), np.ones((1, HEAD_DIM))), F32)
    alog_r = a_log.reshape(2, 1, heads)
    alog_c = a_log.reshape(2, heads, 1)

    def specs(cidx):
        return [
            pl.BlockSpec((None, CHUNK, width), lambda bi, i: (bi, cidx(i), 0)),
            pl.BlockSpec((None, CHUNK, bc_w), lambda bi, i: (bi, cidx(i), b_blk)),
            pl.BlockSpec((None, CHUNK, bc_w), lambda bi, i: (bi, cidx(i), b_blk + 1)),
            pl.BlockSpec((None, CHUNK, LANES), lambda bi, i: (bi, cidx(i), 0)),
            pl.BlockSpec((LANES, CHUNK), lambda bi, i: (0, bi * nc + cidx(i))),
        ]

    fwd = lambda i: i
    bwd = lambda i: nc - 1 - i
    kern = functools.partial(_ssd_kernel, heads=heads)
    return pl.pallas_call(
        kern,
        grid=(b, nc),
        in_specs=specs(fwd) + specs(bwd) + [
            pl.BlockSpec((2, 1, heads), lambda bi, i: (0, 0, 0)),
            pl.BlockSpec((2, heads, 1), lambda bi, i: (0, 0, 0)),
            pl.BlockSpec((heads, width), lambda bi, i: (0, 0)),
        ],
        out_specs=[
            pl.BlockSpec((None, CHUNK, width), lambda bi, i: (bi, fwd(i), 0)),
            pl.BlockSpec((None, CHUNK, width), lambda bi, i: (bi, bwd(i), 0)),
        ],
        out_shape=[jax.ShapeDtypeStruct((b, seq, width), BF16)] * 2,
        scratch_shapes=[pltpu.VMEM((N_BC_GROUPS, D_STATE, gw), F32)] * 2,
        compiler_params=pltpu.CompilerParams(
            dimension_semantics=("arbitrary", "arbitrary"),
            vmem_limit_bytes=VMEM_LIMIT_BYTES),
        name="ssd",
    )(xbc, xbc, xbc, dt, dtt, xbc, xbc, xbc, dt, dtt, alog_r, alog_c, expand)


def _tail_kernel(yf_ref, yb_ref, xs_ref, zs_ref, gf_ref, gs_ref, hm_ref, x_ref,
                 dsk_ref, nw_ref, wbf_ref, wbs_ref, wo_ref, lng_ref, lnb_ref, o_ref, *, alpha):
    y = (yf_ref[...].astype(F32) + yb_ref[...].astype(F32)
         + xs_ref[...].astype(F32) * dsk_ref[...])
    h = y * zs_ref[...].astype(F32)
    width = h.shape[1]
    gw = width // N_BC_GROUPS
    parts = []
    for g in range(N_BC_GROUPS):
        hg = h[:, g * gw:(g + 1) * gw]
        ms = jnp.mean(hg * hg, axis=-1, keepdims=True)
        parts.append(hg * lax.rsqrt(ms + NORM_EPS))
    hn = (jnp.concatenate(parts, axis=1) * nw_ref[...]).astype(BF16)
    p_s = _dot(hn, wbs_ref[...])
    p_f = _dot(hm_ref[...], wbf_ref[...])
    merged = gf_ref[...].astype(F32) * p_f + gs_ref[...].astype(F32) * p_s
    out = _dot(merged.astype(BF16), wo_ref[...])
    r = alpha * x_ref[...] + out
    mu = jnp.mean(r, axis=-1, keepdims=True)
    rc = r - mu
    var = jnp.mean(rc * rc, axis=-1, keepdims=True)
    o_ref[...] = rc * lax.rsqrt(var + NORM_EPS) * lng_ref[...] + lnb_ref[...]


def _tail(yf, yb, xbc, proj, hmix, x2, dsk_e, nw, wbf, wbs, wo, lng, lnb, *, tm, alpha,
          zs_blk, gf_blk, gs_blk):
    t, d = x2.shape
    fw = hmix.shape[1]
    row = lambda shape: pl.BlockSpec(shape, lambda i: (0, 0))
    resident = lambda shape: pl.BlockSpec(shape, lambda i: (0, 0), pipeline_mode=pl.Buffered(1))
    kern = functools.partial(_tail_kernel, alpha=alpha)
    return pl.pallas_call(
        kern,
        grid=(t // tm,),
        in_specs=[
            pl.BlockSpec((tm, d), lambda i: (i, 0)),
            pl.BlockSpec((tm, d), lambda i: (i, 0)),
            pl.BlockSpec((tm, d), lambda i: (i, 0)),
            pl.BlockSpec((tm, d), lambda i: (i, zs_blk)),
            pl.BlockSpec((tm, d), lambda i: (i, gf_blk)),
            pl.BlockSpec((tm, d), lambda i: (i, gs_blk)),
            pl.BlockSpec((tm, fw), lambda i: (i, 0)),
            pl.BlockSpec((tm, d), lambda i: (i, 0)),
            row((1, d)), row((1, d)),
            resident((fw, d)), resident((d, d)), resident((d, d)),
            row((1, d)), row((1, d)),
        ],
        out_specs=pl.BlockSpec((tm, d), lambda i: (i, 0)),
        out_shape=jax.ShapeDtypeStruct((t, d), F32),
        compiler_params=pltpu.CompilerParams(
            dimension_semantics=("arbitrary",),
            vmem_limit_bytes=VMEM_LIMIT_BYTES),
        name="tail",
    )(yf, yb, xbc, proj, proj, proj, hmix, x2, dsk_e, nw, wbf, wbs, wo, lng, lnb)


def _pick_tile(n, target):
    t = min(n, target)
    while n % t:
        t //= 2
    return t


def _layer(x, w_in, b_gate, conv_w, conv_b, dt_bias_fwd, a_log_fwd, dt_bias_bwd, a_log_bwd,
           d_skip, ssd_norm_w, f_mix_w, f_mix_b, w_branch_f, w_branch_s, w_out, ln_g, ln_b, *, depth):
    b, seq, d = x.shape
    t = b * seq
    fw = d // 2
    heads = d // HEAD_DIM
    bc_w = N_BC_GROUPS * D_STATE
    cch = d + 2 * bc_w
    assert fw == F_GROUPS * LANES and seq % (LANES * SUBLANES) == 0 and 2 * heads <= LANES

    o_zf = fw
    o_zs = 2 * fw
    o_xbc = o_zs + d
    o_dtf = o_xbc + cch
    o_gate = o_dtf + 2 * heads
    tn = 1024
    w_main = jnp.concatenate([w_in[:, o_zs:o_xbc], w_in[:, o_gate:], w_in[:, o_xbc:o_dtf],
                              w_in[:, :fw], w_in[:, o_zf:o_zs]], axis=1).astype(BF16)
    nm = w_main.shape[1]
    b_main = jnp.concatenate([jnp.zeros((d,), F32), b_gate.astype(F32),
                              jnp.zeros((nm - 3 * d,), F32)]).reshape(1, nm)
    w_dt = jnp.concatenate([w_in[:, o_dtf:o_gate], jnp.zeros((d, LANES - 2 * heads), w_in.dtype)],
                           axis=1).astype(BF16)
    b_dt = jnp.concatenate([dt_bias_fwd, dt_bias_bwd,
                            jnp.zeros((LANES - 2 * heads,), F32)]).astype(F32).reshape(1, LANES)
    c_gf, c_gs, c_xbc = d, 2 * d, 3 * d
    c_uf = c_xbc + cch
    c_zf = c_uf + fw
    silu_tiles = ((0, d // tn), (c_zf // tn, nm // tn))
    sigmoid_tiles = ((c_gf // tn, c_xbc // tn),)

    x2 = x.reshape(t, d)
    proj, dt, dtt = _inproj(x2, w_main, b_main, w_dt, b_dt, tm=_pick_tile(t, 1024), tn=tn,
                            silu_tiles=silu_tiles, sigmoid_tiles=sigmoid_tiles)
    proj3 = proj.reshape(b, seq, nm)

    hmix = _fnet(proj3, f_mix_w, f_mix_b, seq=seq, uf_blk0=c_uf // LANES, zf_blk0=c_zf // LANES)

    conv_tc = 1024
    xbc = _conv(proj3, conv_w.astype(F32), conv_b.astype(F32).reshape(1, cch), seq=seq,
                tr=_pick_tile(seq, 512), tc=conv_tc, xbc_blk0=c_xbc // conv_tc)

    a_log = jnp.stack([a_log_fwd, a_log_bwd]).astype(F32)
    yf, yb = _ssd(xbc, dt.reshape(b, seq, LANES), dtt, a_log, seq=seq, heads=heads)

    dsk_e = jnp.repeat(d_skip.astype(F32), HEAD_DIM).reshape(1, d)
    alpha = (2.0 * depth) ** 0.25
    out = _tail(yf.reshape(t, d), yb.reshape(t, d), xbc.reshape(t, cch), proj, hmix.reshape(t, fw), x2,
                dsk_e, ssd_norm_w.astype(F32).reshape(1, d),
                w_branch_f.astype(BF16), w_branch_s.astype(BF16), w_out.astype(BF16),
                ln_g.astype(F32).reshape(1, d), ln_b.astype(F32).reshape(1, d),
                tm=_pick_tile(t, 256), alpha=alpha, zs_blk=0, gf_blk=c_gf // d, gs_blk=c_gs // d)
    return out.reshape(b, seq, d)


def kernel(x, w_in, b_gate, conv_w, conv_b, dt_bias_fwd, a_log_fwd, dt_bias_bwd, a_log_bwd, d_skip,
           ssd_norm_w, f_mix_w, f_mix_b, w_branch_f, w_branch_s, w_out, ln_g, ln_b):
    depth = w_in.shape[0]
    for i in range(depth):
        x = _layer(x, w_in[i], b_gate[i], conv_w[i], conv_b[i], dt_bias_fwd[i], a_log_fwd[i],
                   dt_bias_bwd[i], a_log_bwd[i], d_skip[i], ssd_norm_w[i], f_mix_w[i], f_mix_b[i],
                   w_branch_f[i], w_branch_s[i], w_out[i], ln_g[i], ln_b[i], depth=depth)
    return x
```

```python
import functools
import math

import numpy as np
import jax
import jax.numpy as jnp
from jax import lax
from jax.experimental import pallas as pl
from jax.experimental.pallas import tpu as pltpu

F32 = jnp.float32
BF16 = jnp.bfloat16

F_GROUPS = 8
HEAD_DIM = 64
N_BC_GROUPS = 4
D_STATE = 128
D_CONV = 5
CONV_PAD = D_CONV // 2
CHUNK = 128
NORM_EPS = 1e-5

LANES = 128
SUBLANES = 8
BF16_ROWS = 16
VMEM_LIMIT_BYTES = 56 * 1024 * 1024

LOG2E = 1.4426950408889634
FNET_BLOCK = 8
INPROJ_ROWS = 256
INPROJ_CONV_ROWS = 512


def _sigmoid(v):
    return 1.0 / (1.0 + jnp.exp(-v))


def _softplus(v):
    return jnp.maximum(v, 0.0) + jnp.log1p(jnp.exp(-jnp.abs(v)))


def _dot(a, b):
    return jnp.dot(a, b, preferred_element_type=F32)


def _dot_f32_lhs(a, m):
    hi = a.astype(BF16)
    r1 = a - hi.astype(F32)
    mid = r1.astype(BF16)
    lo = (r1 - mid.astype(F32)).astype(BF16)
    return _dot(hi, m) + _dot(mid, m) + _dot(lo, m)


def _select_static(j, table):
    out = jnp.int32(table[-1])
    for idx in range(len(table) - 2, -1, -1):
        out = jnp.where(j == idx, jnp.int32(table[idx]), out)
    return out


def _inproj_kernel(x_ref, xp_ref, xn_ref, wa_ref, wb_ref, bg_ref, cw_ref, cb_ref, wdt_ref, bdt_ref,
                   o_ref, dtt_ref, xb_ref, cs_ref, *, blocks_per_seq, kinds):
    i = pl.program_id(0)
    j = pl.program_id(1)
    tm = x_ref.shape[0]
    halo = BF16_ROWS

    @pl.when(j == 0)
    def _():
        pos = i % blocks_per_seq
        xb_ref[pl.ds(0, halo), :] = jnp.where(pos == 0, 0.0, xp_ref[...]).astype(BF16)
        xb_ref[pl.ds(halo, tm), :] = x_ref[...].astype(BF16)
        xb_ref[pl.ds(halo + tm, halo), :] = jnp.where(pos == blocks_per_seq - 1, 0.0,
                                                      xn_ref[...]).astype(BF16)
        dt = _softplus(_dot(xb_ref[pl.ds(halo, tm), :], wdt_ref[...]) + bdt_ref[...])
        dtt_ref[...] = dt.T

    def is_kind(kind):
        m = None
        for idx, k in enumerate(kinds):
            if k == kind:
                c = j == idx
                m = c if m is None else (m | c)
        return m

    def plain(w_ref, act):
        rc = min(INPROJ_ROWS, tm)
        for r in range(tm // rc):
            acc = _dot(xb_ref[pl.ds(halo + r * rc, rc), :], w_ref[...])
            o_ref[pl.ds(r * rc, rc), :] = act(acc).astype(o_ref.dtype)

    @pl.when(is_kind("id"))
    def _():
        plain(wa_ref, lambda a: a)

    @pl.when(is_kind("silu"))
    def _():
        plain(wa_ref, lambda a: a * _sigmoid(a))

    @pl.when(is_kind("sigmoid"))
    def _():
        plain(wb_ref, lambda a: _sigmoid(a + bg_ref[...]))

    @pl.when(is_kind("conv"))
    def _():
        rc = cs_ref.shape[2] - 2 * halo
        n_slabs = cs_ref.shape[1]
        for r in range(tm // rc):
            acc = _dot(xb_ref[pl.ds(r * rc, rc + 2 * halo), :], wa_ref[...])
            for s in range(n_slabs):
                cs_ref[r, s] = acc[:, s * LANES:(s + 1) * LANES]
            for s in range(n_slabs):
                lanes = pl.ds(s * LANES, LANES)
                out = cb_ref[:, lanes]
                for k in range(D_CONV):
                    out = out + (cs_ref[r, s, pl.ds(halo - CONV_PAD + k, rc), :]
                                 * cw_ref[pl.ds(k, 1), lanes])
                o_ref[pl.ds(r * rc, rc), lanes] = (out * _sigmoid(out)).astype(o_ref.dtype)


def _inproj(x2, w_a, w_b, b_gate, conv_w, conv_b, w_dt, b_dt, *, seq, tm, tn, kinds, out_blocks,
            conv_tile0):
    t, d = x2.shape
    n_a = w_a.shape[1] // tn
    n_b = w_b.shape[1] // tn
    n_conv = conv_w.shape[1] // tn
    hb = tm // BF16_ROWS
    last_hb = t // BF16_ROWS - 1
    conv_rows = min(INPROJ_CONV_ROWS, tm)
    kern = functools.partial(_inproj_kernel, blocks_per_seq=seq // tm, kinds=kinds)
    return pl.pallas_call(
        kern,
        grid=(t // tm, n_a + n_b),
        in_specs=[
            pl.BlockSpec((tm, d), lambda i, j: (i, 0)),
            pl.BlockSpec((BF16_ROWS, d), lambda i, j: (jnp.maximum(i * hb - 1, 0), 0)),
            pl.BlockSpec((BF16_ROWS, d), lambda i, j: (jnp.minimum((i + 1) * hb, last_hb), 0)),
            pl.BlockSpec((d, tn), lambda i, j: (0, jnp.minimum(j, n_a - 1))),
            pl.BlockSpec((d, tn), lambda i, j: (0, jnp.clip(j - n_a, 0, n_b - 1))),
            pl.BlockSpec((1, tn), lambda i, j: (0, jnp.clip(j - n_a, 0, n_b - 1))),
            pl.BlockSpec((D_CONV, tn), lambda i, j: (0, jnp.clip(j - conv_tile0, 0, n_conv - 1))),
            pl.BlockSpec((1, tn), lambda i, j: (0, jnp.clip(j - conv_tile0, 0, n_conv - 1))),
            pl.BlockSpec((d, LANES), lambda i, j: (0, 0)),
            pl.BlockSpec((1, LANES), lambda i, j: (0, 0)),
        ],
        out_specs=[
            pl.BlockSpec((tm, tn), lambda i, j: (i, _select_static(j, out_blocks))),
            pl.BlockSpec((LANES, tm), lambda i, j: (0, i)),
        ],
        out_shape=[
            jax.ShapeDtypeStruct((t, (n_a + n_b) * tn), BF16),
            jax.ShapeDtypeStruct((LANES, t), F32),
        ],
        scratch_shapes=[
            pltpu.VMEM((tm + 2 * BF16_ROWS, d), BF16),
            pltpu.VMEM((tm // conv_rows, tn // LANES, conv_rows + 2 * BF16_ROWS, LANES), F32),
        ],
        compiler_params=pltpu.CompilerParams(
            dimension_semantics=("arbitrary", "arbitrary"),
            vmem_limit_bytes=VMEM_LIMIT_BYTES),
        name="inproj",
    )(x2, x2, x2, w_a, w_b, b_gate, conv_w, conv_b, w_dt, b_dt)


def _fnet_kernel(u_ref, zf_ref, cs_ref, d1_ref, twr_ref, twi_ref, d2_ref, wmix_ref, bmix_ref,
                 o_ref, vr_ref, vi_ref, zr_ref, zi_ref, x_ref, *, n_min, p1, p2, p3, scale):
    seq = u_ref.shape[0]
    n_maj = seq // n_min
    blk = FNET_BLOCK

    slabs0 = min(blk, n_min)
    def s0(i, c):
        r0 = pl.multiple_of(i * (slabs0 * n_maj), slabs0 * n_maj)
        v = _dot(u_ref[pl.ds(r0, slabs0 * n_maj), :], cs_ref[...])
        for s in range(slabs0):
            d0 = pl.multiple_of((i * slabs0 + s) * p1, SUBLANES)
            vr_ref[pl.ds(d0, n_maj), :] = v[s * n_maj:(s + 1) * n_maj, :LANES]
            vi_ref[pl.ds(d0, n_maj), :] = v[s * n_maj:(s + 1) * n_maj, LANES:]
        return c
    lax.fori_loop(0, n_min // slabs0, s0, 0)

    def s1(i, c):
        cols = []
        for j in range(blk):
            n_lo = i * blk + j
            cols.append(jnp.concatenate([vr_ref[pl.ds(n_lo, n_min, stride=p1), :],
                                         vi_ref[pl.ds(n_lo, n_min, stride=p1), :]], axis=0))
        r = jnp.concatenate(cols, axis=1).astype(BF16)
        z = _dot(d1_ref[...], r)
        for j in range(blk):
            n_lo = i * blk + j
            zr = z[:n_min, j * LANES:(j + 1) * LANES]
            zi = z[n_min:, j * LANES:(j + 1) * LANES]
            tr, ti = twr_ref[n_lo], twi_ref[n_lo]
            d0 = pl.multiple_of(n_lo * p2, SUBLANES)
            zr_ref[pl.ds(d0, n_min), :] = zr * tr - zi * ti
            zi_ref[pl.ds(d0, n_min), :] = zr * ti + zi * tr
        return c
    lax.fori_loop(0, n_maj // blk, s1, 0)

    def s2(i, c):
        cols = []
        for j in range(blk):
            k_min = i * blk + j
            cols.append(jnp.concatenate([zr_ref[pl.ds(k_min, n_maj, stride=p2), :],
                                         zi_ref[pl.ds(k_min, n_maj, stride=p2), :]], axis=0))
        r = jnp.concatenate(cols, axis=1).astype(BF16)
        xk = _dot(d2_ref[...], r)
        for j in range(blk):
            x_ref[pl.ds(i * blk + j, n_maj, stride=p3), :] = xk[:, j * LANES:(j + 1) * LANES]
        return c
    lax.fori_loop(0, n_min // blk, s2, 0)

    rows = blk * n_min
    def s3(i, c):
        four = jnp.concatenate(
            [x_ref[pl.ds(pl.multiple_of((i * blk + s) * p3, SUBLANES), n_min), :] for s in range(blk)],
            axis=0)
        mixed = _dot((four * scale).astype(BF16), wmix_ref[...]) + bmix_ref[...]
        r0 = pl.multiple_of(i * rows, rows)
        o_ref[pl.ds(r0, rows), :] = (mixed * zf_ref[pl.ds(r0, rows), :].astype(F32)).astype(o_ref.dtype)
        return c
    lax.fori_loop(0, n_maj // blk, s3, 0)


def _dft_mats(n):
    k = np.arange(n)
    ang = 2.0 * np.pi * ((k[:, None] * k[None, :]) % n) / n
    return np.cos(ang), np.sin(ang)


def _bf16_matrix(a):
    return jnp.asarray(a, F32).astype(BF16)


def _fnet(proj, f_mix_w, f_mix_b, *, seq, uf_blk0, zf_blk0):
    b = proj.shape[0]
    gd = LANES
    n_maj = LANES
    n_min = seq // n_maj
    p1 = n_maj + SUBLANES
    p2 = n_min + SUBLANES
    p3 = n_min + SUBLANES
    c128, s128 = _dft_mats(gd)
    cs = _bf16_matrix(np.concatenate([c128, -s128], axis=1))
    cm, sm = _dft_mats(n_min)
    d1 = _bf16_matrix(np.block([[cm, sm], [-sm, cm]]))
    d2 = _bf16_matrix(np.concatenate([c128, s128], axis=1))
    nl = np.arange(n_maj)[:, None]
    km = np.arange(n_min)[None, :]
    ang = 2.0 * np.pi * ((nl * km) % seq) / seq
    twr = jnp.broadcast_to(jnp.asarray(np.cos(ang), F32)[:, :, None], (n_maj, n_min, LANES))
    twi = jnp.broadcast_to(jnp.asarray(-np.sin(ang), F32)[:, :, None], (n_maj, n_min, LANES))
    wmix = f_mix_w.astype(BF16)
    bmix = f_mix_b.reshape(F_GROUPS, 1, gd).astype(F32)
    scale = 1.0 / math.sqrt(seq * gd)
    assert n_min % FNET_BLOCK == 0
    kern = functools.partial(_fnet_kernel, n_min=n_min, p1=p1, p2=p2, p3=p3, scale=scale)
    const2 = lambda shape: pl.BlockSpec(shape, lambda bi, g: (0, 0))
    return pl.pallas_call(
        kern,
        grid=(b, F_GROUPS),
        in_specs=[
            pl.BlockSpec((None, seq, gd), lambda bi, g: (bi, 0, uf_blk0 + g)),
            pl.BlockSpec((None, seq, gd), lambda bi, g: (bi, 0, zf_blk0 + g)),
            const2((gd, 2 * gd)),
            const2((2 * n_min, 2 * n_min)),
            pl.BlockSpec((n_maj, n_min, LANES), lambda bi, g: (0, 0, 0)),
            pl.BlockSpec((n_maj, n_min, LANES), lambda bi, g: (0, 0, 0)),
            const2((n_maj, 2 * n_maj)),
            pl.BlockSpec((None, gd, gd), lambda bi, g: (g, 0, 0)),
            pl.BlockSpec((None, 1, gd), lambda bi, g: (g, 0, 0)),
        ],
        out_specs=pl.BlockSpec((None, seq, gd), lambda bi, g: (bi, 0, g)),
        out_shape=jax.ShapeDtypeStruct((b, seq, F_GROUPS * gd), BF16),
        scratch_shapes=[
            pltpu.VMEM((n_min * p1, LANES), F32),
            pltpu.VMEM((n_min * p1, LANES), F32),
            pltpu.VMEM((n_maj * p2, LANES), F32),
            pltpu.VMEM((n_maj * p2, LANES), F32),
            pltpu.VMEM((n_maj * p3, LANES), F32),
        ],
        compiler_params=pltpu.CompilerParams(
            dimension_semantics=("arbitrary", "arbitrary"),
            vmem_limit_bytes=VMEM_LIMIT_BYTES),
        name="fnet",
    )(proj, proj, cs, d1, twr, twi, d2, wmix, bmix)


def _ssd_direction(xs_ref, b_ref, c_ref, dtt_ref, acol_ref, st_ref, y_ref, *, reverse, heads, row0):
    lc = CHUNK
    hpg = heads // N_BC_GROUPS
    gw = hpg * HEAD_DIM
    row = lax.broadcasted_iota(jnp.int32, (lc, lc), 0)
    col = lax.broadcasted_iota(jnp.int32, (lc, lc), 1)
    incl = (col >= row) if reverse else (col <= row)
    cum_t = ((row >= col) if reverse else (row <= col)).astype(BF16)

    dtt = dtt_ref[...]
    acs_t = _dot_f32_lhs(dtt * acol_ref[...], cum_t)
    end_c = acs_t[:, 0:1] if reverse else acs_t[:, lc - 1:lc]
    w_t = dtt * jnp.exp(end_c - acs_t)
    r2_t = (acs_t - jnp.log(dtt)) * LOG2E
    acs = acs_t.T
    acs2 = acs * LOG2E
    cd_row = jnp.exp(acs[0:1, :] if reverse else acs[lc - 1:lc, :])

    lane = lax.broadcasted_iota(jnp.int32, (lc, LANES), 1)
    low = lane < HEAD_DIM
    low1 = low[0:1]
    zero = jnp.zeros((), BF16)

    for g in range(N_BC_GROUPS):
        bg = b_ref[:, g * D_STATE:(g + 1) * D_STATE]
        cg = c_ref[:, g * D_STATE:(g + 1) * D_STATE]
        cb = lax.dot_general(cg, bg, (((1,), (1,)), ((), ())), preferred_element_type=F32
                             ).astype(BF16)
        bt = bg.astype(F32).T
        prev = st_ref[g]
        y_off = _dot(cg, prev.astype(BF16))
        for hp in range(hpg // 2):
            h0 = row0 + g * hpg + 2 * hp
            ms, bs, es = [], [], []
            for hr in (h0, h0 + 1):
                colb = jnp.broadcast_to(acs2[:, hr:hr + 1], (lc, lc))
                es.append(jnp.exp2(colb))
                decay = jnp.exp2(colb - r2_t[hr:hr + 1, :]).astype(BF16)
                ms.append(cb * jnp.where(incl, decay, zero))
                bs.append((bt * w_t[hr:hr + 1, :]).astype(BF16))
            lhs = jnp.concatenate([jnp.concatenate(ms, axis=1), jnp.concatenate(bs, axis=1)], axis=0)
            c0 = g * gw + hp * LANES
            xpair = xs_ref[:, c0:c0 + LANES]
            rhs = jnp.concatenate([jnp.where(low, xpair, zero), jnp.where(low, zero, xpair)], axis=0)
            res = _dot(lhs, rhs)
            e_pair = jnp.where(low, es[0], es[1])
            y_ref[:, c0:c0 + LANES] = (res[:lc] + y_off[:, hp * LANES:(hp + 1) * LANES] * e_pair
                                       ).astype(y_ref.dtype)
            cd_pair = jnp.where(low1, jnp.broadcast_to(cd_row[:, h0:h0 + 1], (1, LANES)),
                                jnp.broadcast_to(cd_row[:, h0 + 1:h0 + 2], (1, LANES)))
            st_ref[g, :, hp * LANES:(hp + 1) * LANES] = (
                prev[:, hp * LANES:(hp + 1) * LANES] * cd_pair + res[lc:])


def _ssd_kernel(xsf_ref, bf_ref, cf_ref, dttf_ref, xsb_ref, bb_ref, cb_ref, dttb_ref, acol_ref,
                yf_ref, yb_ref, stf_ref, stb_ref, *, heads):
    @pl.when(pl.program_id(1) == 0)
    def _():
        stf_ref[...] = jnp.zeros_like(stf_ref)
        stb_ref[...] = jnp.zeros_like(stb_ref)

    _ssd_direction(xsf_ref, bf_ref, cf_ref, dttf_ref, acol_ref, stf_ref, yf_ref,
                   reverse=False, heads=heads, row0=0)
    _ssd_direction(xsb_ref, bb_ref, cb_ref, dttb_ref, acol_ref, stb_ref, yb_ref,
                   reverse=True, heads=heads, row0=heads)


def _ssd(proj3, dtt, a_col, *, seq, heads, xs_col0):
    b = proj3.shape[0]
    nc = seq // CHUNK
    width = heads * HEAD_DIM
    bc_w = N_BC_GROUPS * D_STATE
    gw = width // N_BC_GROUPS
    xs_blk = xs_col0 // width
    b_blk = (xs_col0 + width) // bc_w

    def specs(cidx):
        return [
            pl.BlockSpec((None, CHUNK, width), lambda bi, i: (bi, cidx(i), xs_blk)),
            pl.BlockSpec((None, CHUNK, bc_w), lambda bi, i: (bi, cidx(i), b_blk)),
            pl.BlockSpec((None, CHUNK, bc_w), lambda bi, i: (bi, cidx(i), b_blk + 1)),
            pl.BlockSpec((LANES, CHUNK), lambda bi, i: (0, bi * nc + cidx(i))),
        ]

    fwd = lambda i: i
    bwd = lambda i: nc - 1 - i
    kern = functools.partial(_ssd_kernel, heads=heads)
    return pl.pallas_call(
        kern,
        grid=(b, nc),
        in_specs=specs(fwd) + specs(bwd) + [pl.BlockSpec((LANES, 1), lambda bi, i: (0, 0))],
        out_specs=[
            pl.BlockSpec((None, CHUNK, width), lambda bi, i: (bi, fwd(i), 0)),
            pl.BlockSpec((None, CHUNK, width), lambda bi, i: (bi, bwd(i), 0)),
        ],
        out_shape=[jax.ShapeDtypeStruct((b, seq, width), BF16)] * 2,
        scratch_shapes=[pltpu.VMEM((N_BC_GROUPS, D_STATE, gw), F32)] * 2,
        compiler_params=pltpu.CompilerParams(
            dimension_semantics=("arbitrary", "arbitrary"),
            vmem_limit_bytes=VMEM_LIMIT_BYTES),
        name="ssd",
    )(proj3, proj3, proj3, dtt, proj3, proj3, proj3, dtt, a_col)


def _tail_kernel(yf_ref, yb_ref, xs_ref, zs_ref, gf_ref, gs_ref, hm_ref, x_ref,
                 dsk_ref, nw_ref, wbf_ref, wbs_ref, wo_ref, lng_ref, lnb_ref, o_ref, *, alpha):
    y = (yf_ref[...].astype(F32) + yb_ref[...].astype(F32)
         + xs_ref[...].astype(F32) * dsk_ref[...])
    h = y * zs_ref[...].astype(F32)
    width = h.shape[1]
    gw = width // N_BC_GROUPS
    parts = []
    for g in range(N_BC_GROUPS):
        hg = h[:, g * gw:(g + 1) * gw]
        ms = jnp.mean(hg * hg, axis=-1, keepdims=True)
        parts.append(hg * lax.rsqrt(ms + NORM_EPS))
    hn = (jnp.concatenate(parts, axis=1) * nw_ref[...]).astype(BF16)
    p_s = _dot(hn, wbs_ref[...])
    p_f = _dot(hm_ref[...], wbf_ref[...])
    merged = gf_ref[...].astype(F32) * p_f + gs_ref[...].astype(F32) * p_s
    out = _dot(merged.astype(BF16), wo_ref[...])
    r = alpha * x_ref[...] + out
    mu = jnp.mean(r, axis=-1, keepdims=True)
    rc = r - mu
    var = jnp.mean(rc * rc, axis=-1, keepdims=True)
    o_ref[...] = rc * lax.rsqrt(var + NORM_EPS) * lng_ref[...] + lnb_ref[...]


def _tail(yf, yb, proj, hmix, x2, dsk_e, nw, wbf, wbs, wo, lng, lnb, *, tm, alpha,
          xs_blk, zs_blk, gf_blk, gs_blk):
    t, d = x2.shape
    fw = hmix.shape[1]
    row = lambda shape: pl.BlockSpec(shape, lambda i: (0, 0))
    resident = lambda shape: pl.BlockSpec(shape, lambda i: (0, 0), pipeline_mode=pl.Buffered(1))
    kern = functools.partial(_tail_kernel, alpha=alpha)
    return pl.pallas_call(
        kern,
        grid=(t // tm,),
        in_specs=[
            pl.BlockSpec((tm, d), lambda i: (i, 0)),
            pl.BlockSpec((tm, d), lambda i: (i, 0)),
            pl.BlockSpec((tm, d), lambda i: (i, xs_blk)),
            pl.BlockSpec((tm, d), lambda i: (i, zs_blk)),
            pl.BlockSpec((tm, d), lambda i: (i, gf_blk)),
            pl.BlockSpec((tm, d), lambda i: (i, gs_blk)),
            pl.BlockSpec((tm, fw), lambda i: (i, 0)),
            pl.BlockSpec((tm, d), lambda i: (i, 0)),
            row((1, d)), row((1, d)),
            resident((fw, d)), resident((d, d)), resident((d, d)),
            row((1, d)), row((1, d)),
        ],
        out_specs=pl.BlockSpec((tm, d), lambda i: (i, 0)),
        out_shape=jax.ShapeDtypeStruct((t, d), F32),
        compiler_params=pltpu.CompilerParams(
            dimension_semantics=("arbitrary",),
            vmem_limit_bytes=VMEM_LIMIT_BYTES),
        name="tail",
    )(yf, yb, proj, proj, proj, proj, hmix, x2, dsk_e, nw, wbf, wbs, wo, lng, lnb)


def _pick_tile(n, target):
    t = min(n, target)
    while n % t:
        t //= 2
    return t


def _layer(x, w_in, b_gate, conv_w, conv_b, dt_bias_fwd, a_log_fwd, dt_bias_bwd, a_log_bwd,
           d_skip, ssd_norm_w, f_mix_w, f_mix_b, w_branch_f, w_branch_s, w_out, ln_g, ln_b, *, depth):
    b, seq, d = x.shape
    t = b * seq
    fw = d // 2
    heads = d // HEAD_DIM
    bc_w = N_BC_GROUPS * D_STATE
    cch = d + 2 * bc_w
    assert fw == F_GROUPS * LANES and seq % (LANES * SUBLANES) == 0 and 2 * heads <= LANES

    o_dt = 2 * fw + d + cch
    o_gate = o_dt + 2 * heads
    tn = 1024
    w_a = w_in[:, :o_dt].astype(BF16)
    w_b = w_in[:, o_gate:].astype(BF16)
    n_a, n_b = o_dt // tn, (2 * d) // tn
    kinds = (("id",) * (fw // tn) + ("silu",) * ((fw + d) // tn) + ("conv",) * (cch // tn)
             + ("sigmoid",) * n_b)
    c_gf, c_gs, c_xbc = d, 2 * d, 3 * d
    c_uf = c_xbc + cch
    c_zf = c_uf + fw
    src_cols = ([c_uf + k * tn for k in range(fw // tn)] + [c_zf + k * tn for k in range(fw // tn)]
                + [k * tn for k in range(d // tn)] + [c_xbc + k * tn for k in range(cch // tn)]
                + [c_gf + k * tn for k in range(n_b)])
    out_blocks = tuple(c // tn for c in src_cols)
    conv_tile0 = (2 * fw + d) // tn

    w_dt = jnp.concatenate([w_in[:, o_dt:o_gate], jnp.zeros((d, LANES - 2 * heads), w_in.dtype)],
                           axis=1).astype(BF16)
    b_dt = jnp.concatenate([dt_bias_fwd, dt_bias_bwd,
                            jnp.zeros((LANES - 2 * heads,), F32)]).astype(F32).reshape(1, LANES)

    x2 = x.reshape(t, d)
    proj, dtt = _inproj(x2, w_a, w_b, b_gate.astype(F32).reshape(1, 2 * d), conv_w.astype(F32),
                        conv_b.astype(F32).reshape(1, cch), w_dt, b_dt, seq=seq,
                        tm=_pick_tile(seq, 1024), tn=tn, kinds=kinds, out_blocks=out_blocks,
                        conv_tile0=conv_tile0)
    nm = proj.shape[1]
    proj3 = proj.reshape(b, seq, nm)

    hmix = _fnet(proj3, f_mix_w, f_mix_b, seq=seq, uf_blk0=c_uf // LANES, zf_blk0=c_zf // LANES)

    a_col = -jnp.exp(jnp.concatenate([a_log_fwd, a_log_bwd,
                                      jnp.zeros((LANES - 2 * heads,), F32)]).astype(F32)
                     ).reshape(LANES, 1)
    yf, yb = _ssd(proj3, dtt, a_col, seq=seq, heads=heads, xs_col0=c_xbc)

    dsk_e = jnp.repeat(d_skip.astype(F32), HEAD_DIM).reshape(1, d)
    alpha = (2.0 * depth) ** 0.25
    out = _tail(yf.reshape(t, d), yb.reshape(t, d), proj, hmix.reshape(t, fw), x2,
                dsk_e, ssd_norm_w.astype(F32).reshape(1, d),
                w_branch_f.astype(BF16), w_branch_s.astype(BF16), w_out.astype(BF16),
                ln_g.astype(F32).reshape(1, d), ln_b.astype(F32).reshape(1, d),
                tm=_pick_tile(t, 256), alpha=alpha, xs_blk=c_xbc // d, zs_blk=0,
                gf_blk=c_gf // d, gs_blk=c_gs // d)
    return out.reshape(b, seq, d)


def kernel(x, w_in, b_gate, conv_w, conv_b, dt_bias_fwd, a_log_fwd, dt_bias_bwd, a_log_bwd, d_skip,
           ssd_norm_w, f_mix_w, f_mix_b, w_branch_f, w_branch_s, w_out, ln_g, ln_b):
    depth = w_in.shape[0]
    for i in range(depth):
        x = _layer(x, w_in[i], b_gate[i], conv_w[i], conv_b[i], dt_bias_fwd[i], a_log_fwd[i],
                   dt_bias_bwd[i], a_log_bwd[i], d_skip[i], ssd_norm_w[i], f_mix_w[i], f_mix_b[i],
                   w_branch_f[i], w_branch_s[i], w_out[i], ln_g[i], ln_b[i], depth=depth)
    return x
```

```python
import functools
import math

import numpy as np
import jax
import jax.numpy as jnp
from jax import lax
from jax.experimental import pallas as pl
from jax.experimental.pallas import tpu as pltpu

F32 = jnp.float32
BF16 = jnp.bfloat16

F_GROUPS = 8
HEAD_DIM = 64
N_BC_GROUPS = 4
D_STATE = 128
D_CONV = 5
CONV_PAD = D_CONV // 2
CHUNK = 128
NORM_EPS = 1e-5

LANES = 128
SUBLANES = 8
BF16_ROWS = 16
VMEM_LIMIT_BYTES = 56 * 1024 * 1024

LOG2E = 1.4426950408889634
FNET_BLOCK = 8
INPROJ_ROWS = 256
INPROJ_CONV_ROWS = 512


def _sigmoid(v):
    return 1.0 / (1.0 + jnp.exp(-v))


def _softplus(v):
    return jnp.maximum(v, 0.0) + jnp.log1p(jnp.exp(-jnp.abs(v)))


def _dot(a, b):
    return jnp.dot(a, b, preferred_element_type=F32)


def _dot_f32_lhs(a, m):
    hi = a.astype(BF16)
    r1 = a - hi.astype(F32)
    mid = r1.astype(BF16)
    lo = (r1 - mid.astype(F32)).astype(BF16)
    return _dot(hi, m) + _dot(mid, m) + _dot(lo, m)


def _select_static(j, table):
    out = jnp.int32(table[-1])
    for idx in range(len(table) - 2, -1, -1):
        out = jnp.where(j == idx, jnp.int32(table[idx]), out)
    return out


def _inproj_kernel(x_ref, xp_ref, xn_ref, wa_ref, wb_ref, bg_ref, cw_ref, cb_ref, wdt_ref, bdt_ref,
                   o_ref, dtt_ref, xb_ref, cs_ref, *, blocks_per_seq, kinds):
    i = pl.program_id(0)
    j = pl.program_id(1)
    tm = x_ref.shape[0]
    halo = BF16_ROWS

    @pl.when(j == 0)
    def _():
        pos = i % blocks_per_seq
        xb_ref[pl.ds(0, halo), :] = jnp.where(pos == 0, 0.0, xp_ref[...]).astype(BF16)
        xb_ref[pl.ds(halo, tm), :] = x_ref[...].astype(BF16)
        xb_ref[pl.ds(halo + tm, halo), :] = jnp.where(pos == blocks_per_seq - 1, 0.0,
                                                      xn_ref[...]).astype(BF16)
        dt = _softplus(_dot(xb_ref[pl.ds(halo, tm), :], wdt_ref[...]) + bdt_ref[...])
        dtt_ref[...] = dt.T

    def is_kind(kind):
        m = None
        for idx, k in enumerate(kinds):
            if k == kind:
                c = j == idx
                m = c if m is None else (m | c)
        return m

    def plain(w_ref, act):
        rc = min(INPROJ_ROWS, tm)
        for r in range(tm // rc):
            acc = _dot(xb_ref[pl.ds(halo + r * rc, rc), :], w_ref[...])
            o_ref[pl.ds(r * rc, rc), :] = act(acc).astype(o_ref.dtype)

    @pl.when(is_kind("id"))
    def _():
        plain(wa_ref, lambda a: a)

    @pl.when(is_kind("silu"))
    def _():
        plain(wa_ref, lambda a: a * _sigmoid(a))

    @pl.when(is_kind("sigmoid"))
    def _():
        plain(wb_ref, lambda a: _sigmoid(a + bg_ref[...]))

    @pl.when(is_kind("conv"))
    def _():
        rc = cs_ref.shape[2] - 2 * halo
        n_slabs = cs_ref.shape[1]
        for r in range(tm // rc):
            acc = _dot(xb_ref[pl.ds(r * rc, rc + 2 * halo), :], wa_ref[...])
            for s in range(n_slabs):
                cs_ref[r, s] = acc[:, s * LANES:(s + 1) * LANES]
            for s in range(n_slabs):
                lanes = pl.ds(s * LANES, LANES)
                out = cb_ref[:, lanes]
                for k in range(D_CONV):
                    out = out + (cs_ref[r, s, pl.ds(halo - CONV_PAD + k, rc), :]
                                 * cw_ref[pl.ds(k, 1), lanes])
                o_ref[pl.ds(r * rc, rc), lanes] = (out * _sigmoid(out)).astype(o_ref.dtype)


def _inproj(x2, w_a, w_b, b_gate, conv_w, conv_b, w_dt, b_dt, *, seq, tm, tn, kinds, out_blocks,
            conv_tile0):
    t, d = x2.shape
    n_b = w_b.shape[1] // tn
    n_a = len(kinds) - n_b
    n_conv = conv_w.shape[1] // tn
    hb = tm // BF16_ROWS
    last_hb = t // BF16_ROWS - 1
    conv_rows = min(INPROJ_CONV_ROWS, tm)
    kern = functools.partial(_inproj_kernel, blocks_per_seq=seq // tm, kinds=kinds)
    return pl.pallas_call(
        kern,
        grid=(t // tm, n_a + n_b),
        in_specs=[
            pl.BlockSpec((tm, d), lambda i, j: (i, 0)),
            pl.BlockSpec((BF16_ROWS, d), lambda i, j: (jnp.maximum(i * hb - 1, 0), 0)),
            pl.BlockSpec((BF16_ROWS, d), lambda i, j: (jnp.minimum((i + 1) * hb, last_hb), 0)),
            pl.BlockSpec((d, tn), lambda i, j: (0, jnp.minimum(j, n_a - 1))),
            pl.BlockSpec((d, tn), lambda i, j: (0, jnp.clip(j - n_a, 0, n_b - 1))),
            pl.BlockSpec((1, tn), lambda i, j: (0, jnp.clip(j - n_a, 0, n_b - 1))),
            pl.BlockSpec((D_CONV, tn), lambda i, j: (0, jnp.clip(j - conv_tile0, 0, n_conv - 1))),
            pl.BlockSpec((1, tn), lambda i, j: (0, jnp.clip(j - conv_tile0, 0, n_conv - 1))),
            pl.BlockSpec((d, LANES), lambda i, j: (0, 0)),
            pl.BlockSpec((1, LANES), lambda i, j: (0, 0)),
        ],
        out_specs=[
            pl.BlockSpec((tm, tn), lambda i, j: (i, _select_static(j, out_blocks))),
            pl.BlockSpec((LANES, tm), lambda i, j: (0, i)),
        ],
        out_shape=[
            jax.ShapeDtypeStruct((t, (n_a + n_b) * tn), BF16),
            jax.ShapeDtypeStruct((LANES, t), F32),
        ],
        scratch_shapes=[
            pltpu.VMEM((tm + 2 * BF16_ROWS, d), BF16),
            pltpu.VMEM((tm // conv_rows, tn // LANES, conv_rows + 2 * BF16_ROWS, LANES), F32),
        ],
        compiler_params=pltpu.CompilerParams(
            dimension_semantics=("arbitrary", "arbitrary"),
            vmem_limit_bytes=VMEM_LIMIT_BYTES),
        name="inproj",
    )(x2, x2, x2, w_a, w_b, b_gate, conv_w, conv_b, w_dt, b_dt)


def _fnet_kernel(u_ref, zf_ref, cs_ref, d1_ref, twr_ref, twi_ref, d2_ref, wmix_ref, bmix_ref,
                 o_ref, vr_ref, vi_ref, zr_ref, zi_ref, x_ref, *, n_min, p1, p2, p3, scale):
    seq = u_ref.shape[0]
    n_maj = seq // n_min
    blk = FNET_BLOCK

    slabs0 = min(blk, n_min)
    def s0(i, c):
        r0 = pl.multiple_of(i * (slabs0 * n_maj), slabs0 * n_maj)
        v = _dot(u_ref[pl.ds(r0, slabs0 * n_maj), :], cs_ref[...])
        for s in range(slabs0):
            d0 = pl.multiple_of((i * slabs0 + s) * p1, SUBLANES)
            vr_ref[pl.ds(d0, n_maj), :] = v[s * n_maj:(s + 1) * n_maj, :LANES]
            vi_ref[pl.ds(d0, n_maj), :] = v[s * n_maj:(s + 1) * n_maj, LANES:]
        return c
    lax.fori_loop(0, n_min // slabs0, s0, 0, unroll=min(4, n_min // slabs0))

    def s1(i, c):
        cols = []
        for j in range(blk):
            n_lo = i * blk + j
            cols.append(jnp.concatenate([vr_ref[pl.ds(n_lo, n_min, stride=p1), :],
                                         vi_ref[pl.ds(n_lo, n_min, stride=p1), :]], axis=0))
        r = jnp.concatenate(cols, axis=1).astype(BF16)
        z = _dot(d1_ref[...], r)
        for j in range(blk):
            n_lo = i * blk + j
            zr = z[:n_min, j * LANES:(j + 1) * LANES]
            zi = z[n_min:, j * LANES:(j + 1) * LANES]
            tr, ti = twr_ref[n_lo], twi_ref[n_lo]
            d0 = pl.multiple_of(n_lo * p2, SUBLANES)
            zr_ref[pl.ds(d0, n_min), :] = zr * tr - zi * ti
            zi_ref[pl.ds(d0, n_min), :] = zr * ti + zi * tr
        return c
    lax.fori_loop(0, n_maj // blk, s1, 0, unroll=2)

    def s2(i, c):
        cols = []
        for j in range(blk):
            k_min = i * blk + j
            cols.append(jnp.concatenate([zr_ref[pl.ds(k_min, n_maj, stride=p2), :],
                                         zi_ref[pl.ds(k_min, n_maj, stride=p2), :]], axis=0))
        r = jnp.concatenate(cols, axis=1).astype(BF16)
        xk = _dot(d2_ref[...], r)
        for j in range(blk):
            x_ref[pl.ds(i * blk + j, n_maj, stride=p3), :] = xk[:, j * LANES:(j + 1) * LANES]
        return c
    lax.fori_loop(0, n_min // blk, s2, 0, unroll=min(2, n_min // blk))

    rows = blk * n_min
    def s3(i, c):
        four = jnp.concatenate(
            [x_ref[pl.ds(pl.multiple_of((i * blk + s) * p3, SUBLANES), n_min), :] for s in range(blk)],
            axis=0)
        mixed = _dot((four * scale).astype(BF16), wmix_ref[...]) + bmix_ref[...]
        r0 = pl.multiple_of(i * rows, rows)
        o_ref[pl.ds(r0, rows), :] = (mixed * zf_ref[pl.ds(r0, rows), :].astype(F32)).astype(o_ref.dtype)
        return c
    lax.fori_loop(0, n_maj // blk, s3, 0, unroll=4)


def _dft_mats(n):
    k = np.arange(n)
    ang = 2.0 * np.pi * ((k[:, None] * k[None, :]) % n) / n
    return np.cos(ang), np.sin(ang)


def _bf16_matrix(a):
    return jnp.asarray(a, F32).astype(BF16)


def _fnet(proj, f_mix_w, f_mix_b, *, seq, uf_blk0, zf_blk0):
    b = proj.shape[0]
    gd = LANES
    n_maj = LANES
    n_min = seq // n_maj
    p1 = n_maj + SUBLANES
    p2 = n_min + SUBLANES
    p3 = n_min + SUBLANES
    c128, s128 = _dft_mats(gd)
    cs = _bf16_matrix(np.concatenate([c128, -s128], axis=1))
    cm, sm = _dft_mats(n_min)
    d1 = _bf16_matrix(np.block([[cm, sm], [-sm, cm]]))
    d2 = _bf16_matrix(np.concatenate([c128, s128], axis=1))
    nl = np.arange(n_maj)[:, None]
    km = np.arange(n_min)[None, :]
    ang = 2.0 * np.pi * ((nl * km) % seq) / seq
    twr = jnp.broadcast_to(jnp.asarray(np.cos(ang), F32)[:, :, None], (n_maj, n_min, LANES))
    twi = jnp.broadcast_to(jnp.asarray(-np.sin(ang), F32)[:, :, None], (n_maj, n_min, LANES))
    wmix = f_mix_w.astype(BF16)
    bmix = f_mix_b.reshape(F_GROUPS, 1, gd).astype(F32)
    scale = 1.0 / math.sqrt(seq * gd)
    assert n_min % FNET_BLOCK == 0
    kern = functools.partial(_fnet_kernel, n_min=n_min, p1=p1, p2=p2, p3=p3, scale=scale)
    const2 = lambda shape: pl.BlockSpec(shape, lambda bi, g: (0, 0))
    return pl.pallas_call(
        kern,
        grid=(b, F_GROUPS),
        in_specs=[
            pl.BlockSpec((None, seq, gd), lambda bi, g: (bi, 0, uf_blk0 + g)),
            pl.BlockSpec((None, seq, gd), lambda bi, g: (bi, 0, zf_blk0 + g)),
            const2((gd, 2 * gd)),
            const2((2 * n_min, 2 * n_min)),
            pl.BlockSpec((n_maj, n_min, LANES), lambda bi, g: (0, 0, 0)),
            pl.BlockSpec((n_maj, n_min, LANES), lambda bi, g: (0, 0, 0)),
            const2((n_maj, 2 * n_maj)),
            pl.BlockSpec((None, gd, gd), lambda bi, g: (g, 0, 0)),
            pl.BlockSpec((None, 1, gd), lambda bi, g: (g, 0, 0)),
        ],
        out_specs=pl.BlockSpec((None, seq, gd), lambda bi, g: (bi, 0, g)),
        out_shape=jax.ShapeDtypeStruct((b, seq, F_GROUPS * gd), BF16),
        scratch_shapes=[
            pltpu.VMEM((n_min * p1, LANES), F32),
            pltpu.VMEM((n_min * p1, LANES), F32),
            pltpu.VMEM((n_maj * p2, LANES), F32),
            pltpu.VMEM((n_maj * p2, LANES), F32),
            pltpu.VMEM((n_maj * p3, LANES), F32),
        ],
        compiler_params=pltpu.CompilerParams(
            dimension_semantics=("arbitrary", "arbitrary"),
            vmem_limit_bytes=VMEM_LIMIT_BYTES),
        name="fnet",
    )(proj, proj, cs, d1, twr, twi, d2, wmix, bmix)


def _ssd_direction(xs_ref, b_ref, c_ref, dtt_ref, acol_ref, st_ref, y_ref, *, reverse, heads, row0):
    lc = CHUNK
    hpg = heads // N_BC_GROUPS
    gw = hpg * HEAD_DIM
    row = lax.broadcasted_iota(jnp.int32, (lc, lc), 0)
    col = lax.broadcasted_iota(jnp.int32, (lc, lc), 1)
    incl = (col >= row) if reverse else (col <= row)
    cum_t = ((row >= col) if reverse else (row <= col)).astype(BF16)

    dtt = dtt_ref[...]
    acs_t = _dot_f32_lhs(dtt * acol_ref[...], cum_t)
    end_c = acs_t[:, 0:1] if reverse else acs_t[:, lc - 1:lc]
    w_t = dtt * jnp.exp(end_c - acs_t)
    r2_t = (acs_t - jnp.log(dtt)) * LOG2E
    acs = acs_t.T
    acs2 = acs * LOG2E
    cd_row = jnp.exp(acs[0:1, :] if reverse else acs[lc - 1:lc, :])

    lane = lax.broadcasted_iota(jnp.int32, (lc, LANES), 1)
    low = lane < HEAD_DIM
    low1 = low[0:1]
    zero = jnp.zeros((), BF16)

    for g in range(N_BC_GROUPS):
        bg = b_ref[:, g * D_STATE:(g + 1) * D_STATE]
        cg = c_ref[:, g * D_STATE:(g + 1) * D_STATE]
        cb = lax.dot_general(cg, bg, (((1,), (1,)), ((), ())), preferred_element_type=F32
                             ).astype(BF16)
        bt = bg.astype(F32).T
        prev = st_ref[g]
        y_off = _dot(cg, prev.astype(BF16))
        for hp in range(hpg // 2):
            h0 = row0 + g * hpg + 2 * hp
            ms, bs, colbs = [], [], []
            for hr in (h0, h0 + 1):
                colb = jnp.broadcast_to(acs2[:, hr:hr + 1], (lc, lc))
                colbs.append(colb)
                decay = jnp.exp2(colb - r2_t[hr:hr + 1, :]).astype(BF16)
                ms.append(cb * jnp.where(incl, decay, zero))
                bs.append((bt * w_t[hr:hr + 1, :]).astype(BF16))
            lhs = jnp.concatenate([jnp.concatenate(ms, axis=1), jnp.concatenate(bs, axis=1)], axis=0)
            c0 = g * gw + hp * LANES
            xpair = xs_ref[:, c0:c0 + LANES]
            rhs = jnp.concatenate([jnp.where(low, xpair, zero), jnp.where(low, zero, xpair)], axis=0)
            res = _dot(lhs, rhs)
            e_pair = jnp.exp2(jnp.where(low, colbs[0], colbs[1]))
            y_ref[:, c0:c0 + LANES] = (res[:lc] + y_off[:, hp * LANES:(hp + 1) * LANES] * e_pair
                                       ).astype(y_ref.dtype)
            cd_pair = jnp.where(low1, jnp.broadcast_to(cd_row[:, h0:h0 + 1], (1, LANES)),
                                jnp.broadcast_to(cd_row[:, h0 + 1:h0 + 2], (1, LANES)))
            st_ref[g, :, hp * LANES:(hp + 1) * LANES] = (
                prev[:, hp * LANES:(hp + 1) * LANES] * cd_pair + res[lc:])


def _ssd_kernel(xsf_ref, bf_ref, cf_ref, dttf_ref, xsb_ref, bb_ref, cb_ref, dttb_ref, acol_ref,
                yf_ref, yb_ref, stf_ref, stb_ref, *, heads):
    @pl.when(pl.program_id(1) == 0)
    def _():
        stf_ref[...] = jnp.zeros_like(stf_ref)
        stb_ref[...] = jnp.zeros_like(stb_ref)

    _ssd_direction(xsf_ref, bf_ref, cf_ref, dttf_ref, acol_ref, stf_ref, yf_ref,
                   reverse=False, heads=heads, row0=0)
    _ssd_direction(xsb_ref, bb_ref, cb_ref, dttb_ref, acol_ref, stb_ref, yb_ref,
                   reverse=True, heads=heads, row0=heads)


def _ssd(proj3, dtt, a_col, *, seq, heads, xs_col0):
    b = proj3.shape[0]
    nc = seq // CHUNK
    width = heads * HEAD_DIM
    bc_w = N_BC_GROUPS * D_STATE
    gw = width // N_BC_GROUPS
    xs_blk = xs_col0 // width
    b_blk = (xs_col0 + width) // bc_w

    def specs(cidx):
        return [
            pl.BlockSpec((None, CHUNK, width), lambda bi, i: (bi, cidx(i), xs_blk)),
            pl.BlockSpec((None, CHUNK, bc_w), lambda bi, i: (bi, cidx(i), b_blk)),
            pl.BlockSpec((None, CHUNK, bc_w), lambda bi, i: (bi, cidx(i), b_blk + 1)),
            pl.BlockSpec((LANES, CHUNK), lambda bi, i: (0, bi * nc + cidx(i))),
        ]

    fwd = lambda i: i
    bwd = lambda i: nc - 1 - i
    kern = functools.partial(_ssd_kernel, heads=heads)
    return pl.pallas_call(
        kern,
        grid=(b, nc),
        in_specs=specs(fwd) + specs(bwd) + [pl.BlockSpec((LANES, 1), lambda bi, i: (0, 0))],
        out_specs=[
            pl.BlockSpec((None, CHUNK, width), lambda bi, i: (bi, fwd(i), 0)),
            pl.BlockSpec((None, CHUNK, width), lambda bi, i: (bi, bwd(i), 0)),
        ],
        out_shape=[jax.ShapeDtypeStruct((b, seq, width), BF16)] * 2,
        scratch_shapes=[pltpu.VMEM((N_BC_GROUPS, D_STATE, gw), F32)] * 2,
        compiler_params=pltpu.CompilerParams(
            dimension_semantics=("arbitrary", "arbitrary"),
            vmem_limit_bytes=VMEM_LIMIT_BYTES),
        name="ssd",
    )(proj3, proj3, proj3, dtt, proj3, proj3, proj3, dtt, a_col)


def _tail_kernel(yf_ref, yb_ref, xs_ref, zs_ref, gf_ref, gs_ref, hm_ref, x_ref,
                 dsk_ref, nw_ref, wbf_ref, wbs_ref, wo_ref, lng_ref, lnb_ref, o_ref, *, alpha):
    y = (yf_ref[...].astype(F32) + yb_ref[...].astype(F32)
         + xs_ref[...].astype(F32) * dsk_ref[...])
    h = y * zs_ref[...].astype(F32)
    width = h.shape[1]
    gw = width // N_BC_GROUPS
    parts = []
    for g in range(N_BC_GROUPS):
        hg = h[:, g * gw:(g + 1) * gw]
        ms = jnp.mean(hg * hg, axis=-1, keepdims=True)
        parts.append(hg * lax.rsqrt(ms + NORM_EPS))
    hn = (jnp.concatenate(parts, axis=1) * nw_ref[...]).astype(BF16)
    p_s = _dot(hn, wbs_ref[...])
    p_f = _dot(hm_ref[...], wbf_ref[...])
    merged = gf_ref[...].astype(F32) * p_f + gs_ref[...].astype(F32) * p_s
    out = _dot(merged.astype(BF16), wo_ref[...])
    r = alpha * x_ref[...] + out
    mu = jnp.mean(r, axis=-1, keepdims=True)
    rc = r - mu
    var = jnp.mean(rc * rc, axis=-1, keepdims=True)
    o_ref[...] = rc * lax.rsqrt(var + NORM_EPS) * lng_ref[...] + lnb_ref[...]


def _tail(yf, yb, proj, hmix, x2, dsk_e, nw, wbf, wbs, wo, lng, lnb, *, tm, alpha,
          xs_blk, zs_blk, gf_blk, gs_blk):
    t, d = x2.shape
    fw = hmix.shape[1]
    row = lambda shape: pl.BlockSpec(shape, lambda i: (0, 0))
    resident = lambda shape: pl.BlockSpec(shape, lambda i: (0, 0), pipeline_mode=pl.Buffered(1))
    kern = functools.partial(_tail_kernel, alpha=alpha)
    return pl.pallas_call(
        kern,
        grid=(t // tm,),
        in_specs=[
            pl.BlockSpec((tm, d), lambda i: (i, 0)),
            pl.BlockSpec((tm, d), lambda i: (i, 0)),
            pl.BlockSpec((tm, d), lambda i: (i, xs_blk)),
            pl.BlockSpec((tm, d), lambda i: (i, zs_blk)),
            pl.BlockSpec((tm, d), lambda i: (i, gf_blk)),
            pl.BlockSpec((tm, d), lambda i: (i, gs_blk)),
            pl.BlockSpec((tm, fw), lambda i: (i, 0)),
            pl.BlockSpec((tm, d), lambda i: (i, 0)),
            row((1, d)), row((1, d)),
            resident((fw, d)), resident((d, d)), resident((d, d)),
            row((1, d)), row((1, d)),
        ],
        out_specs=pl.BlockSpec((tm, d), lambda i: (i, 0)),
        out_shape=jax.ShapeDtypeStruct((t, d), F32),
        compiler_params=pltpu.CompilerParams(
            dimension_semantics=("arbitrary",),
            vmem_limit_bytes=VMEM_LIMIT_BYTES),
        name="tail",
    )(yf, yb, proj, proj, proj, proj, hmix, x2, dsk_e, nw, wbf, wbs, wo, lng, lnb)


def _pick_tile(n, target):
    t = min(n, target)
    while n % t:
        t //= 2
    return t


def _layer(x, w_in, b_gate, conv_w, conv_b, dt_bias_fwd, a_log_fwd, dt_bias_bwd, a_log_bwd,
           d_skip, ssd_norm_w, f_mix_w, f_mix_b, w_branch_f, w_branch_s, w_out, ln_g, ln_b, *, depth):
    b, seq, d = x.shape
    t = b * seq
    fw = d // 2
    heads = d // HEAD_DIM
    bc_w = N_BC_GROUPS * D_STATE
    cch = d + 2 * bc_w
    assert fw == F_GROUPS * LANES and seq % (LANES * SUBLANES) == 0 and 2 * heads <= LANES

    o_dt = 2 * fw + d + cch
    o_gate = o_dt + 2 * heads
    tn = 1024
    w_a = w_in.astype(BF16)
    w_b = w_a[:, o_gate:]
    n_a, n_b = o_dt // tn, (2 * d) // tn
    kinds = (("id",) * (fw // tn) + ("silu",) * ((fw + d) // tn) + ("conv",) * (cch // tn)
             + ("sigmoid",) * n_b)
    c_gf, c_gs, c_xbc = d, 2 * d, 3 * d
    c_uf = c_xbc + cch
    c_zf = c_uf + fw
    src_cols = ([c_uf + k * tn for k in range(fw // tn)] + [c_zf + k * tn for k in range(fw // tn)]
                + [k * tn for k in range(d // tn)] + [c_xbc + k * tn for k in range(cch // tn)]
                + [c_gf + k * tn for k in range(n_b)])
    out_blocks = tuple(c // tn for c in src_cols)
    conv_tile0 = (2 * fw + d) // tn

    w_dt = jnp.concatenate([w_in[:, o_dt:o_gate], jnp.zeros((d, LANES - 2 * heads), w_in.dtype)],
                           axis=1).astype(BF16)
    b_dt = jnp.concatenate([dt_bias_fwd, dt_bias_bwd,
                            jnp.zeros((LANES - 2 * heads,), F32)]).astype(F32).reshape(1, LANES)

    x2 = x.reshape(t, d)
    proj, dtt = _inproj(x2, w_a, w_b, b_gate.astype(F32).reshape(1, 2 * d), conv_w.astype(F32),
                        conv_b.astype(F32).reshape(1, cch), w_dt, b_dt, seq=seq,
                        tm=_pick_tile(seq, 1024), tn=tn, kinds=kinds, out_blocks=out_blocks,
                        conv_tile0=conv_tile0)
    nm = proj.shape[1]
    proj3 = proj.reshape(b, seq, nm)

    hmix = _fnet(proj3, f_mix_w, f_mix_b, seq=seq, uf_blk0=c_uf // LANES, zf_blk0=c_zf // LANES)

    a_col = -jnp.exp(jnp.concatenate([a_log_fwd, a_log_bwd,
                                      jnp.zeros((LANES - 2 * heads,), F32)]).astype(F32)
                     ).reshape(LANES, 1)
    yf, yb = _ssd(proj3, dtt, a_col, seq=seq, heads=heads, xs_col0=c_xbc)

    dsk_e = jnp.repeat(d_skip.astype(F32), HEAD_DIM).reshape(1, d)
    alpha = (2.0 * depth) ** 0.25
    out = _tail(yf.reshape(t, d), yb.reshape(t, d), proj, hmix.reshape(t, fw), x2,
                dsk_e, ssd_norm_w.astype(F32).reshape(1, d),
                w_branch_f.astype(BF16), w_branch_s.astype(BF16), w_out.astype(BF16),
                ln_g.astype(F32).reshape(1, d), ln_b.astype(F32).reshape(1, d),
                tm=_pick_tile(t, 256), alpha=alpha, xs_blk=c_xbc // d, zs_blk=0,
                gf_blk=c_gf // d, gs_blk=c_gs // d)
    return out.reshape(b, seq, d)


def kernel(x, w_in, b_gate, conv_w, conv_b, dt_bias_fwd, a_log_fwd, dt_bias_bwd, a_log_bwd, d_skip,
           ssd_norm_w, f_mix_w, f_mix_b, w_branch_f, w_branch_s, w_out, ln_g, ln_b):
    depth = w_in.shape[0]
    for i in range(depth):
        x = _layer(x, w_in[i], b_gate[i], conv_w[i], conv_b[i], dt_bias_fwd[i], a_log_fwd[i],
                   dt_bias_bwd[i], a_log_bwd[i], d_skip[i], ssd_norm_w[i], f_mix_w[i], f_mix_b[i],
                   w_branch_f[i], w_branch_s[i], w_out[i], ln_g[i], ln_b[i], depth=depth)
    return x
```

```python
import functools
import math

import numpy as np
import jax
import jax.numpy as jnp
from jax import lax
from jax.experimental import pallas as pl
from jax.experimental.pallas import tpu as pltpu

F32 = jnp.float32
BF16 = jnp.bfloat16

F_GROUPS = 8
HEAD_DIM = 64
N_BC_GROUPS = 4
D_STATE = 128
D_CONV = 5
CONV_PAD = D_CONV // 2
CHUNK = 128
NORM_EPS = 1e-5

LANES = 128
SUBLANES = 8
BF16_ROWS = 16
VMEM_LIMIT_BYTES = 56 * 1024 * 1024

LOG2E = 1.4426950408889634
FNET_BLOCK = 8
INPROJ_ROWS = 256
INPROJ_CONV_ROWS = 512


def _sigmoid(v):
    return 1.0 / (1.0 + jnp.exp(-v))


def _softplus(v):
    return jnp.maximum(v, 0.0) + jnp.log1p(jnp.exp(-jnp.abs(v)))


def _dot(a, b):
    return jnp.dot(a, b, preferred_element_type=F32)


def _dot_f32_lhs(a, m):
    hi = a.astype(BF16)
    r1 = a - hi.astype(F32)
    mid = r1.astype(BF16)
    lo = (r1 - mid.astype(F32)).astype(BF16)
    return _dot(hi, m) + _dot(mid, m) + _dot(lo, m)


def _select_static(j, table):
    out = jnp.int32(table[-1])
    for idx in range(len(table) - 2, -1, -1):
        out = jnp.where(j == idx, jnp.int32(table[idx]), out)
    return out


def _inproj_kernel(x_ref, xp_ref, xn_ref, wa_ref, wb_ref, bg_ref, cw_ref, cb_ref, wdt_ref, bdt_ref,
                   o_ref, dtt_ref, xb_ref, cs_ref, *, blocks_per_seq, kinds):
    i = pl.program_id(0)
    j = pl.program_id(1)
    tm = x_ref.shape[0]
    halo = BF16_ROWS

    @pl.when(j == 0)
    def _():
        pos = i % blocks_per_seq
        xb_ref[pl.ds(0, halo), :] = jnp.where(pos == 0, 0.0, xp_ref[...]).astype(BF16)
        xb_ref[pl.ds(halo, tm), :] = x_ref[...].astype(BF16)
        xb_ref[pl.ds(halo + tm, halo), :] = jnp.where(pos == blocks_per_seq - 1, 0.0,
                                                      xn_ref[...]).astype(BF16)
        dt = _softplus(_dot(xb_ref[pl.ds(halo, tm), :], wdt_ref[...]) + bdt_ref[...])
        dtt_ref[...] = dt.T

    def is_kind(kind):
        m = None
        for idx, k in enumerate(kinds):
            if k == kind:
                c = j == idx
                m = c if m is None else (m | c)
        return m

    def plain(w_ref, act):
        rc = min(INPROJ_ROWS, tm)
        for r in range(tm // rc):
            acc = _dot(xb_ref[pl.ds(halo + r * rc, rc), :], w_ref[...])
            o_ref[pl.ds(r * rc, rc), :] = act(acc).astype(o_ref.dtype)

    @pl.when(is_kind("id"))
    def _():
        plain(wa_ref, lambda a: a)

    @pl.when(is_kind("silu"))
    def _():
        plain(wa_ref, lambda a: a * _sigmoid(a))

    @pl.when(is_kind("sigmoid"))
    def _():
        plain(wb_ref, lambda a: _sigmoid(a + bg_ref[...]))

    @pl.when(is_kind("conv"))
    def _():
        rc = cs_ref.shape[2] - 2 * halo
        n_slabs = cs_ref.shape[1]
        for r in range(tm // rc):
            acc = _dot(xb_ref[pl.ds(r * rc, rc + 2 * halo), :], wa_ref[...])
            for s in range(n_slabs):
                cs_ref[r, s] = acc[:, s * LANES:(s + 1) * LANES]
            for s in range(n_slabs):
                lanes = pl.ds(s * LANES, LANES)
                out = cb_ref[:, lanes]
                for k in range(D_CONV):
                    out = out + (cs_ref[r, s, pl.ds(halo - CONV_PAD + k, rc), :]
                                 * cw_ref[pl.ds(k, 1), lanes])
                o_ref[pl.ds(r * rc, rc), lanes] = (out * _sigmoid(out)).astype(o_ref.dtype)


def _inproj(x2, w_a, w_b, b_gate, conv_w, conv_b, w_dt, b_dt, *, seq, tm, tn, kinds, out_blocks):
    t, d = x2.shape
    n_tiles = len(kinds)
    n_b = w_b.shape[1] // tn
    n_conv = conv_w.shape[1] // tn
    n_pre = n_tiles - n_b - n_conv
    assert kinds[n_pre:n_pre + n_b] == ("sigmoid",) * n_b and kinds[n_pre + n_b:] == ("conv",) * n_conv
    hb = tm // BF16_ROWS
    last_hb = t // BF16_ROWS - 1
    conv_rows = min(INPROJ_CONV_ROWS, tm)
    wa_blk = lambda j: jnp.where(j < n_pre, j, jnp.maximum(j - n_b, n_pre - 1))
    wb_blk = lambda j: jnp.where(j < n_pre, n_b - 1, jnp.minimum(j - n_pre, n_b - 1))
    conv_blk = lambda j: jnp.where(j < n_pre + n_b, n_conv - 1, j - n_pre - n_b)
    kern = functools.partial(_inproj_kernel, blocks_per_seq=seq // tm, kinds=kinds)
    return pl.pallas_call(
        kern,
        grid=(t // tm, n_tiles),
        in_specs=[
            pl.BlockSpec((tm, d), lambda i, j: (i, 0)),
            pl.BlockSpec((BF16_ROWS, d), lambda i, j: (jnp.maximum(i * hb - 1, 0), 0)),
            pl.BlockSpec((BF16_ROWS, d), lambda i, j: (jnp.minimum((i + 1) * hb, last_hb), 0)),
            pl.BlockSpec((d, tn), lambda i, j: (0, wa_blk(j))),
            pl.BlockSpec((d, tn), lambda i, j: (0, wb_blk(j))),
            pl.BlockSpec((1, tn), lambda i, j: (0, wb_blk(j))),
            pl.BlockSpec((D_CONV, tn), lambda i, j: (0, conv_blk(j))),
            pl.BlockSpec((1, tn), lambda i, j: (0, conv_blk(j))),
            pl.BlockSpec((d, LANES), lambda i, j: (0, 0)),
            pl.BlockSpec((1, LANES), lambda i, j: (0, 0)),
        ],
        out_specs=[
            pl.BlockSpec((tm, tn), lambda i, j: (i, _select_static(j, out_blocks))),
            pl.BlockSpec((LANES, tm), lambda i, j: (0, i)),
        ],
        out_shape=[
            jax.ShapeDtypeStruct((t, n_tiles * tn), BF16),
            jax.ShapeDtypeStruct((LANES, t), F32),
        ],
        scratch_shapes=[
            pltpu.VMEM((tm + 2 * BF16_ROWS, d), BF16),
            pltpu.VMEM((tm // conv_rows, tn // LANES, conv_rows + 2 * BF16_ROWS, LANES), F32),
        ],
        compiler_params=pltpu.CompilerParams(
            dimension_semantics=("arbitrary", "arbitrary"),
            vmem_limit_bytes=VMEM_LIMIT_BYTES),
        name="inproj",
    )(x2, x2, x2, w_a, w_b, b_gate, conv_w, conv_b, w_dt, b_dt)


def _fnet_kernel(u_ref, zf_ref, cs_ref, d1_ref, twr_ref, twi_ref, d2_ref, wmix_ref, bmix_ref,
                 o_ref, vr_ref, vi_ref, zr_ref, zi_ref, x_ref, *, n_min, p1, p2, p3, scale):
    seq = u_ref.shape[0]
    n_maj = seq // n_min
    blk = FNET_BLOCK

    slabs0 = min(blk, n_min)
    def s0(i, c):
        r0 = pl.multiple_of(i * (slabs0 * n_maj), slabs0 * n_maj)
        v = _dot(u_ref[pl.ds(r0, slabs0 * n_maj), :], cs_ref[...])
        for s in range(slabs0):
            d0 = pl.multiple_of((i * slabs0 + s) * p1, SUBLANES)
            vr_ref[pl.ds(d0, n_maj), :] = v[s * n_maj:(s + 1) * n_maj, :LANES]
            vi_ref[pl.ds(d0, n_maj), :] = v[s * n_maj:(s + 1) * n_maj, LANES:]
        return c
    lax.fori_loop(0, n_min // slabs0, s0, 0, unroll=min(4, n_min // slabs0))

    def s1(i, c):
        cols = []
        for j in range(blk):
            n_lo = i * blk + j
            cols.append(jnp.concatenate([vr_ref[pl.ds(n_lo, n_min, stride=p1), :],
                                         vi_ref[pl.ds(n_lo, n_min, stride=p1), :]], axis=0))
        r = jnp.concatenate(cols, axis=1).astype(BF16)
        z = _dot(d1_ref[...], r)
        for j in range(blk):
            n_lo = i * blk + j
            zr = z[:n_min, j * LANES:(j + 1) * LANES]
            zi = z[n_min:, j * LANES:(j + 1) * LANES]
            tr, ti = twr_ref[n_lo], twi_ref[n_lo]
            d0 = pl.multiple_of(n_lo * p2, SUBLANES)
            zr_ref[pl.ds(d0, n_min), :] = zr * tr - zi * ti
            zi_ref[pl.ds(d0, n_min), :] = zr * ti + zi * tr
        return c
    lax.fori_loop(0, n_maj // blk, s1, 0, unroll=2)

    def s2(i, c):
        cols = []
        for j in range(blk):
            k_min = i * blk + j
            cols.append(jnp.concatenate([zr_ref[pl.ds(k_min, n_maj, stride=p2), :],
                                         zi_ref[pl.ds(k_min, n_maj, stride=p2), :]], axis=0))
        r = jnp.concatenate(cols, axis=1).astype(BF16)
        xk = _dot(d2_ref[...], r)
        for j in range(blk):
            x_ref[pl.ds(i * blk + j, n_maj, stride=p3), :] = xk[:, j * LANES:(j + 1) * LANES]
        return c
    lax.fori_loop(0, n_min // blk, s2, 0, unroll=min(2, n_min // blk))

    rows = blk * n_min
    def s3(i, c):
        four = jnp.concatenate(
            [x_ref[pl.ds(pl.multiple_of((i * blk + s) * p3, SUBLANES), n_min), :] for s in range(blk)],
            axis=0)
        mixed = _dot((four * scale).astype(BF16), wmix_ref[...]) + bmix_ref[...]
        r0 = pl.multiple_of(i * rows, rows)
        o_ref[pl.ds(r0, rows), :] = (mixed * zf_ref[pl.ds(r0, rows), :].astype(F32)).astype(o_ref.dtype)
        return c
    lax.fori_loop(0, n_maj // blk, s3, 0, unroll=4)


def _dft_mats(n):
    k = np.arange(n)
    ang = 2.0 * np.pi * ((k[:, None] * k[None, :]) % n) / n
    return np.cos(ang), np.sin(ang)


def _bf16_matrix(a):
    return jnp.asarray(a, F32).astype(BF16)


def _fnet(proj, f_mix_w, f_mix_b, *, seq, uf_blk0, zf_blk0):
    b = proj.shape[0]
    gd = LANES
    n_maj = LANES
    n_min = seq // n_maj
    p1 = n_maj + SUBLANES
    p2 = n_min + SUBLANES
    p3 = n_min + SUBLANES
    c128, s128 = _dft_mats(gd)
    cs = _bf16_matrix(np.concatenate([c128, -s128], axis=1))
    cm, sm = _dft_mats(n_min)
    d1 = _bf16_matrix(np.block([[cm, sm], [-sm, cm]]))
    d2 = _bf16_matrix(np.concatenate([c128, s128], axis=1))
    nl = np.arange(n_maj)[:, None]
    km = np.arange(n_min)[None, :]
    ang = 2.0 * np.pi * ((nl * km) % seq) / seq
    twr = jnp.broadcast_to(jnp.asarray(np.cos(ang), F32)[:, :, None], (n_maj, n_min, LANES))
    twi = jnp.broadcast_to(jnp.asarray(-np.sin(ang), F32)[:, :, None], (n_maj, n_min, LANES))
    wmix = f_mix_w.astype(BF16)
    bmix = f_mix_b.reshape(F_GROUPS, 1, gd).astype(F32)
    scale = 1.0 / math.sqrt(seq * gd)
    assert n_min % FNET_BLOCK == 0
    kern = functools.partial(_fnet_kernel, n_min=n_min, p1=p1, p2=p2, p3=p3, scale=scale)
    const2 = lambda shape: pl.BlockSpec(shape, lambda bi, g: (0, 0))
    return pl.pallas_call(
        kern,
        grid=(b, F_GROUPS),
        in_specs=[
            pl.BlockSpec((None, seq, gd), lambda bi, g: (bi, 0, uf_blk0 + g)),
            pl.BlockSpec((None, seq, gd), lambda bi, g: (bi, 0, zf_blk0 + g)),
            const2((gd, 2 * gd)),
            const2((2 * n_min, 2 * n_min)),
            pl.BlockSpec((n_maj, n_min, LANES), lambda bi, g: (0, 0, 0)),
            pl.BlockSpec((n_maj, n_min, LANES), lambda bi, g: (0, 0, 0)),
            const2((n_maj, 2 * n_maj)),
            pl.BlockSpec((None, gd, gd), lambda bi, g: (g, 0, 0)),
            pl.BlockSpec((None, 1, gd), lambda bi, g: (g, 0, 0)),
        ],
        out_specs=pl.BlockSpec((None, seq, gd), lambda bi, g: (bi, 0, g)),
        out_shape=jax.ShapeDtypeStruct((b, seq, F_GROUPS * gd), BF16),
        scratch_shapes=[
            pltpu.VMEM((n_min * p1, LANES), F32),
            pltpu.VMEM((n_min * p1, LANES), F32),
            pltpu.VMEM((n_maj * p2, LANES), F32),
            pltpu.VMEM((n_maj * p2, LANES), F32),
            pltpu.VMEM((n_maj * p3, LANES), F32),
        ],
        compiler_params=pltpu.CompilerParams(
            dimension_semantics=("arbitrary", "arbitrary"),
            vmem_limit_bytes=VMEM_LIMIT_BYTES),
        name="fnet",
    )(proj, proj, cs, d1, twr, twi, d2, wmix, bmix)


def _ssd_direction(xs_ref, b_ref, c_ref, dtt_ref, acol_ref, st_ref, y_ref, *, reverse, heads, row0):
    lc = CHUNK
    hpg = heads // N_BC_GROUPS
    gw = hpg * HEAD_DIM
    row = lax.broadcasted_iota(jnp.int32, (lc, lc), 0)
    col = lax.broadcasted_iota(jnp.int32, (lc, lc), 1)
    incl = (col >= row) if reverse else (col <= row)
    cum_t = ((row >= col) if reverse else (row <= col)).astype(BF16)

    dtt = dtt_ref[...]
    acs_t = _dot_f32_lhs(dtt * acol_ref[...], cum_t)
    end_c = acs_t[:, 0:1] if reverse else acs_t[:, lc - 1:lc]
    w_t = dtt * jnp.exp(end_c - acs_t)
    r2_t = (acs_t - jnp.log(dtt)) * LOG2E
    acs = acs_t.T
    acs2 = acs * LOG2E
    cd_row = jnp.exp(acs[0:1, :] if reverse else acs[lc - 1:lc, :])

    lane = lax.broadcasted_iota(jnp.int32, (lc, LANES), 1)
    low = lane < HEAD_DIM
    low1 = low[0:1]
    zero = jnp.zeros((), BF16)

    for g in range(N_BC_GROUPS):
        bg = b_ref[:, g * D_STATE:(g + 1) * D_STATE]
        cg = c_ref[:, g * D_STATE:(g + 1) * D_STATE]
        cb = lax.dot_general(cg, bg, (((1,), (1,)), ((), ())), preferred_element_type=F32
                             ).astype(BF16)
        bt = bg.astype(F32).T
        prev = st_ref[g]
        y_off = _dot(cg, prev.astype(BF16))
        for hp in range(hpg // 2):
            h0 = row0 + g * hpg + 2 * hp
            ms, bs, colbs = [], [], []
            for hr in (h0, h0 + 1):
                colb = jnp.broadcast_to(acs2[:, hr:hr + 1], (lc, lc))
                colbs.append(colb)
                decay = jnp.exp2(colb - r2_t[hr:hr + 1, :]).astype(BF16)
                ms.append(cb * jnp.where(incl, decay, zero))
                bs.append((bt * w_t[hr:hr + 1, :]).astype(BF16))
            lhs = jnp.concatenate([jnp.concatenate(ms, axis=1), jnp.concatenate(bs, axis=1)], axis=0)
            c0 = g * gw + hp * LANES
            xpair = xs_ref[:, c0:c0 + LANES]
            rhs = jnp.concatenate([jnp.where(low, xpair, zero), jnp.where(low, zero, xpair)], axis=0)
            res = _dot(lhs, rhs)
            e_pair = jnp.exp2(jnp.where(low, colbs[0], colbs[1]))
            y_ref[:, c0:c0 + LANES] = (res[:lc] + y_off[:, hp * LANES:(hp + 1) * LANES] * e_pair
                                       ).astype(y_ref.dtype)
            cd_pair = jnp.where(low1, jnp.broadcast_to(cd_row[:, h0:h0 + 1], (1, LANES)),
                                jnp.broadcast_to(cd_row[:, h0 + 1:h0 + 2], (1, LANES)))
            st_ref[g, :, hp * LANES:(hp + 1) * LANES] = (
                prev[:, hp * LANES:(hp + 1) * LANES] * cd_pair + res[lc:])


def _ssd_kernel(xsf_ref, bf_ref, cf_ref, dttf_ref, xsb_ref, bb_ref, cb_ref, dttb_ref, acol_ref,
                yf_ref, yb_ref, stf_ref, stb_ref, *, heads):
    @pl.when(pl.program_id(1) == 0)
    def _():
        stf_ref[...] = jnp.zeros_like(stf_ref)
        stb_ref[...] = jnp.zeros_like(stb_ref)

    _ssd_direction(xsf_ref, bf_ref, cf_ref, dttf_ref, acol_ref, stf_ref, yf_ref,
                   reverse=False, heads=heads, row0=0)
    _ssd_direction(xsb_ref, bb_ref, cb_ref, dttb_ref, acol_ref, stb_ref, yb_ref,
                   reverse=True, heads=heads, row0=heads)


def _ssd(proj3, dtt, a_col, *, seq, heads, xs_col0):
    b = proj3.shape[0]
    nc = seq // CHUNK
    width = heads * HEAD_DIM
    bc_w = N_BC_GROUPS * D_STATE
    gw = width // N_BC_GROUPS
    xs_blk = xs_col0 // width
    b_blk = (xs_col0 + width) // bc_w

    def specs(cidx):
        return [
            pl.BlockSpec((None, CHUNK, width), lambda bi, i: (bi, cidx(i), xs_blk)),
            pl.BlockSpec((None, CHUNK, bc_w), lambda bi, i: (bi, cidx(i), b_blk)),
            pl.BlockSpec((None, CHUNK, bc_w), lambda bi, i: (bi, cidx(i), b_blk + 1)),
            pl.BlockSpec((LANES, CHUNK), lambda bi, i: (0, bi * nc + cidx(i))),
        ]

    fwd = lambda i: i
    bwd = lambda i: nc - 1 - i
    kern = functools.partial(_ssd_kernel, heads=heads)
    return pl.pallas_call(
        kern,
        grid=(b, nc),
        in_specs=specs(fwd) + specs(bwd) + [pl.BlockSpec((LANES, 1), lambda bi, i: (0, 0))],
        out_specs=[
            pl.BlockSpec((None, CHUNK, width), lambda bi, i: (bi, fwd(i), 0)),
            pl.BlockSpec((None, CHUNK, width), lambda bi, i: (bi, bwd(i), 0)),
        ],
        out_shape=[jax.ShapeDtypeStruct((b, seq, width), BF16)] * 2,
        scratch_shapes=[pltpu.VMEM((N_BC_GROUPS, D_STATE, gw), F32)] * 2,
        compiler_params=pltpu.CompilerParams(
            dimension_semantics=("arbitrary", "arbitrary"),
            vmem_limit_bytes=VMEM_LIMIT_BYTES),
        name="ssd",
    )(proj3, proj3, proj3, dtt, proj3, proj3, proj3, dtt, a_col)


def _tail_kernel(yf_ref, yb_ref, xs_ref, zs_ref, gf_ref, gs_ref, hm_ref, x_ref,
                 dsk_ref, nw_ref, wbf_ref, wbs_ref, wo_ref, lng_ref, lnb_ref, o_ref, *, alpha):
    y = (yf_ref[...].astype(F32) + yb_ref[...].astype(F32)
         + xs_ref[...].astype(F32) * dsk_ref[...])
    h = y * zs_ref[...].astype(F32)
    width = h.shape[1]
    gw = width // N_BC_GROUPS
    parts = []
    for g in range(N_BC_GROUPS):
        hg = h[:, g * gw:(g + 1) * gw]
        ms = jnp.mean(hg * hg, axis=-1, keepdims=True)
        parts.append(hg * lax.rsqrt(ms + NORM_EPS))
    hn = (jnp.concatenate(parts, axis=1) * nw_ref[...]).astype(BF16)
    p_s = _dot(hn, wbs_ref[...])
    p_f = _dot(hm_ref[...], wbf_ref[...])
    merged = gf_ref[...].astype(F32) * p_f + gs_ref[...].astype(F32) * p_s
    out = _dot(merged.astype(BF16), wo_ref[...])
    r = alpha * x_ref[...] + out
    mu = jnp.mean(r, axis=-1, keepdims=True)
    rc = r - mu
    var = jnp.mean(rc * rc, axis=-1, keepdims=True)
    o_ref[...] = rc * lax.rsqrt(var + NORM_EPS) * lng_ref[...] + lnb_ref[...]


def _tail(yf, yb, proj, hmix, x2, dsk_e, nw, wbf, wbs, wo, lng, lnb, *, tm, alpha,
          xs_blk, zs_blk, gf_blk, gs_blk):
    t, d = x2.shape
    fw = hmix.shape[1]
    row = lambda shape: pl.BlockSpec(shape, lambda i: (0, 0))
    resident = lambda shape: pl.BlockSpec(shape, lambda i: (0, 0), pipeline_mode=pl.Buffered(1))
    kern = functools.partial(_tail_kernel, alpha=alpha)
    return pl.pallas_call(
        kern,
        grid=(t // tm,),
        in_specs=[
            pl.BlockSpec((tm, d), lambda i: (i, 0)),
            pl.BlockSpec((tm, d), lambda i: (i, 0)),
            pl.BlockSpec((tm, d), lambda i: (i, xs_blk)),
            pl.BlockSpec((tm, d), lambda i: (i, zs_blk)),
            pl.BlockSpec((tm, d), lambda i: (i, gf_blk)),
            pl.BlockSpec((tm, d), lambda i: (i, gs_blk)),
            pl.BlockSpec((tm, fw), lambda i: (i, 0)),
            pl.BlockSpec((tm, d), lambda i: (i, 0)),
            row((1, d)), row((1, d)),
            resident((fw, d)), resident((d, d)), resident((d, d)),
            row((1, d)), row((1, d)),
        ],
        out_specs=pl.BlockSpec((tm, d), lambda i: (i, 0)),
        out_shape=jax.ShapeDtypeStruct((t, d), F32),
        compiler_params=pltpu.CompilerParams(
            dimension_semantics=("arbitrary",),
            vmem_limit_bytes=VMEM_LIMIT_BYTES),
        name="tail",
    )(yf, yb, proj, proj, proj, proj, hmix, x2, dsk_e, nw, wbf, wbs, wo, lng, lnb)


def _pick_tile(n, target):
    t = min(n, target)
    while n % t:
        t //= 2
    return t


def _layer(x, w_in, b_gate, conv_w, conv_b, dt_bias_fwd, a_log_fwd, dt_bias_bwd, a_log_bwd,
           d_skip, ssd_norm_w, f_mix_w, f_mix_b, w_branch_f, w_branch_s, w_out, ln_g, ln_b, *, depth):
    b, seq, d = x.shape
    t = b * seq
    fw = d // 2
    heads = d // HEAD_DIM
    bc_w = N_BC_GROUPS * D_STATE
    cch = d + 2 * bc_w
    assert fw == F_GROUPS * LANES and seq % (LANES * SUBLANES) == 0 and 2 * heads <= LANES

    o_dt = 2 * fw + d + cch
    o_gate = o_dt + 2 * heads
    tn = 1024
    w_a = w_in.astype(BF16)
    w_b = w_a[:, o_gate:]
    n_b = (2 * d) // tn
    kinds = (("id",) * (fw // tn) + ("silu",) * ((fw + d) // tn) + ("sigmoid",) * n_b
             + ("conv",) * (cch // tn))
    c_gf, c_gs, c_xbc = d, 2 * d, 3 * d
    c_uf = c_xbc + cch
    c_zf = c_uf + fw
    dst_cols = ([c_uf + k * tn for k in range(fw // tn)] + [c_zf + k * tn for k in range(fw // tn)]
                + [k * tn for k in range(d // tn)] + [c_gf + k * tn for k in range(n_b)]
                + [c_xbc + k * tn for k in range(cch // tn)])
    out_blocks = tuple(c // tn for c in dst_cols)

    w_dt = jnp.concatenate([w_in[:, o_dt:o_gate], jnp.zeros((d, LANES - 2 * heads), w_in.dtype)],
                           axis=1).astype(BF16)
    b_dt = jnp.concatenate([dt_bias_fwd, dt_bias_bwd,
                            jnp.zeros((LANES - 2 * heads,), F32)]).astype(F32).reshape(1, LANES)

    x2 = x.reshape(t, d)
    proj, dtt = _inproj(x2, w_a, w_b, b_gate.astype(F32).reshape(1, 2 * d), conv_w.astype(F32),
                        conv_b.astype(F32).reshape(1, cch), w_dt, b_dt, seq=seq,
                        tm=_pick_tile(seq, 1024), tn=tn, kinds=kinds, out_blocks=out_blocks)
    nm = proj.shape[1]
    proj3 = proj.reshape(b, seq, nm)

    hmix = _fnet(proj3, f_mix_w, f_mix_b, seq=seq, uf_blk0=c_uf // LANES, zf_blk0=c_zf // LANES)

    a_col = -jnp.exp(jnp.concatenate([a_log_fwd, a_log_bwd,
                                      jnp.zeros((LANES - 2 * heads,), F32)]).astype(F32)
                     ).reshape(LANES, 1)
    yf, yb = _ssd(proj3, dtt, a_col, seq=seq, heads=heads, xs_col0=c_xbc)

    dsk_e = jnp.repeat(d_skip.astype(F32), HEAD_DIM).reshape(1, d)
    alpha = (2.0 * depth) ** 0.25
    out = _tail(yf.reshape(t, d), yb.reshape(t, d), proj, hmix.reshape(t, fw), x2,
                dsk_e, ssd_norm_w.astype(F32).reshape(1, d),
                w_branch_f.astype(BF16), w_branch_s.astype(BF16), w_out.astype(BF16),
                ln_g.astype(F32).reshape(1, d), ln_b.astype(F32).reshape(1, d),
                tm=_pick_tile(t, 256), alpha=alpha, xs_blk=c_xbc // d, zs_blk=0,
                gf_blk=c_gf // d, gs_blk=c_gs // d)
    return out.reshape(b, seq, d)


def kernel(x, w_in, b_gate, conv_w, conv_b, dt_bias_fwd, a_log_fwd, dt_bias_bwd, a_log_bwd, d_skip,
           ssd_norm_w, f_mix_w, f_mix_b, w_branch_f, w_branch_s, w_out, ln_g, ln_b):
    depth = w_in.shape[0]
    for i in range(depth):
        x = _layer(x, w_in[i], b_gate[i], conv_w[i], conv_b[i], dt_bias_fwd[i], a_log_fwd[i],
                   dt_bias_bwd[i], a_log_bwd[i], d_skip[i], ssd_norm_w[i], f_mix_w[i], f_mix_b[i],
                   w_branch_f[i], w_branch_s[i], w_out[i], ln_g[i], ln_b[i], depth=depth)
    return x
```

```python
import functools
import math

import numpy as np
import jax
import jax.numpy as jnp
from jax import lax
from jax.experimental import pallas as pl
from jax.experimental.pallas import tpu as pltpu

F32 = jnp.float32
BF16 = jnp.bfloat16

F_GROUPS = 8
HEAD_DIM = 64
N_BC_GROUPS = 4
D_STATE = 128
D_CONV = 5
CONV_PAD = D_CONV // 2
CHUNK = 128
NORM_EPS = 1e-5

LANES = 128
SUBLANES = 8
BF16_ROWS = 16
VMEM_LIMIT_BYTES = 56 * 1024 * 1024

LOG2E = 1.4426950408889634
FNET_BLOCK = 8
INPROJ_ROWS = 256
INPROJ_CONV_ROWS = 512


def _sigmoid(v):
    return 1.0 / (1.0 + jnp.exp(-v))


def _softplus(v):
    return jnp.maximum(v, 0.0) + jnp.log1p(jnp.exp(-jnp.abs(v)))


def _dot(a, b):
    return jnp.dot(a, b, preferred_element_type=F32)


def _dot_f32_lhs(a, m):
    hi = a.astype(BF16)
    r1 = a - hi.astype(F32)
    mid = r1.astype(BF16)
    lo = (r1 - mid.astype(F32)).astype(BF16)
    return _dot(hi, m) + _dot(mid, m) + _dot(lo, m)


def _select_static(j, table):
    out = jnp.int32(table[-1])
    for idx in range(len(table) - 2, -1, -1):
        out = jnp.where(j == idx, jnp.int32(table[idx]), out)
    return out


def _inproj_kernel(x_ref, xp_ref, xn_ref, wa_ref, wb_ref, bg_ref, cw_ref, cb_ref, wdt_ref, bdt_ref,
                   o_ref, dtt_ref, xb_ref, cs_ref, *, blocks_per_seq, kinds):
    i = pl.program_id(0)
    j = pl.program_id(1)
    tm = x_ref.shape[0]
    halo = BF16_ROWS

    @pl.when(j == 0)
    def _():
        pos = i % blocks_per_seq
        xb_ref[pl.ds(0, halo), :] = jnp.where(pos == 0, 0.0, xp_ref[...]).astype(BF16)
        xb_ref[pl.ds(halo, tm), :] = x_ref[...].astype(BF16)
        xb_ref[pl.ds(halo + tm, halo), :] = jnp.where(pos == blocks_per_seq - 1, 0.0,
                                                      xn_ref[...]).astype(BF16)
        dt = _softplus(_dot(xb_ref[pl.ds(halo, tm), :], wdt_ref[...]) + bdt_ref[...])
        dtt_ref[...] = dt.T

    def is_kind(kind):
        m = None
        for idx, k in enumerate(kinds):
            if k == kind:
                c = j == idx
                m = c if m is None else (m | c)
        return m

    def plain(w_ref, act):
        rc = min(INPROJ_ROWS, tm)
        for r in range(tm // rc):
            acc = _dot(xb_ref[pl.ds(halo + r * rc, rc), :], w_ref[...])
            o_ref[pl.ds(r * rc, rc), :] = act(acc).astype(o_ref.dtype)

    @pl.when(is_kind("id"))
    def _():
        plain(wa_ref, lambda a: a)

    @pl.when(is_kind("silu"))
    def _():
        plain(wa_ref, lambda a: a * _sigmoid(a))

    @pl.when(is_kind("sigmoid"))
    def _():
        plain(wb_ref, lambda a: _sigmoid(a + bg_ref[...]))

    @pl.when(is_kind("conv"))
    def _():
        rc = cs_ref.shape[2] - 2 * halo
        n_slabs = cs_ref.shape[1]
        for r in range(tm // rc):
            acc = _dot(xb_ref[pl.ds(r * rc, rc + 2 * halo), :], wa_ref[...])
            for s in range(n_slabs):
                cs_ref[r, s] = acc[:, s * LANES:(s + 1) * LANES]
            for s in range(n_slabs):
                lanes = pl.ds(s * LANES, LANES)
                out = cb_ref[:, lanes]
                for k in range(D_CONV):
                    out = out + (cs_ref[r, s, pl.ds(halo - CONV_PAD + k, rc), :]
                                 * cw_ref[pl.ds(k, 1), lanes])
                o_ref[pl.ds(r * rc, rc), lanes] = (out * _sigmoid(out)).astype(o_ref.dtype)


def _inproj(x2, w_a, w_b, b_gate, conv_w, conv_b, w_dt, b_dt, *, seq, tm, tn, kinds, out_blocks):
    t, d = x2.shape
    n_tiles = len(kinds)
    n_b = w_b.shape[1] // tn
    n_conv = conv_w.shape[1] // tn
    n_pre = n_tiles - n_b - n_conv
    assert kinds[n_pre:n_pre + n_b] == ("sigmoid",) * n_b and kinds[n_pre + n_b:] == ("conv",) * n_conv
    hb = tm // BF16_ROWS
    last_hb = t // BF16_ROWS - 1
    conv_rows = min(INPROJ_CONV_ROWS, tm)
    wa_blk = lambda j: jnp.where(j < n_pre, j, jnp.maximum(j - n_b, n_pre - 1))
    wb_blk = lambda j: jnp.where(j < n_pre, n_b - 1, jnp.minimum(j - n_pre, n_b - 1))
    conv_blk = lambda j: jnp.where(j < n_pre + n_b, n_conv - 1, j - n_pre - n_b)
    kern = functools.partial(_inproj_kernel, blocks_per_seq=seq // tm, kinds=kinds)
    return pl.pallas_call(
        kern,
        grid=(t // tm, n_tiles),
        in_specs=[
            pl.BlockSpec((tm, d), lambda i, j: (i, 0)),
            pl.BlockSpec((BF16_ROWS, d), lambda i, j: (jnp.maximum(i * hb - 1, 0), 0)),
            pl.BlockSpec((BF16_ROWS, d), lambda i, j: (jnp.minimum((i + 1) * hb, last_hb), 0)),
            pl.BlockSpec((d, tn), lambda i, j: (0, wa_blk(j))),
            pl.BlockSpec((d, tn), lambda i, j: (0, wb_blk(j))),
            pl.BlockSpec((1, tn), lambda i, j: (0, wb_blk(j))),
            pl.BlockSpec((D_CONV, tn), lambda i, j: (0, conv_blk(j))),
            pl.BlockSpec((1, tn), lambda i, j: (0, conv_blk(j))),
            pl.BlockSpec((d, LANES), lambda i, j: (0, 0)),
            pl.BlockSpec((1, LANES), lambda i, j: (0, 0)),
        ],
        out_specs=[
            pl.BlockSpec((tm, tn), lambda i, j: (i, _select_static(j, out_blocks))),
            pl.BlockSpec((LANES, tm), lambda i, j: (0, i)),
        ],
        out_shape=[
            jax.ShapeDtypeStruct((t, n_tiles * tn), BF16),
            jax.ShapeDtypeStruct((LANES, t), F32),
        ],
        scratch_shapes=[
            pltpu.VMEM((tm + 2 * BF16_ROWS, d), BF16),
            pltpu.VMEM((tm // conv_rows, tn // LANES, conv_rows + 2 * BF16_ROWS, LANES), F32),
        ],
        compiler_params=pltpu.CompilerParams(
            dimension_semantics=("arbitrary", "arbitrary"),
            vmem_limit_bytes=VMEM_LIMIT_BYTES),
        name="inproj",
    )(x2, x2, x2, w_a, w_b, b_gate, conv_w, conv_b, w_dt, b_dt)


def _fnet_kernel(u_ref, zf_ref, cs_ref, d1_ref, twr_ref, twi_ref, d2_ref, wmix_ref, bmix_ref,
                 o_ref, vr_ref, vi_ref, zr_ref, zi_ref, x_ref, *, n_min, p1, p2, p3, scale):
    seq = u_ref.shape[0]
    n_maj = seq // n_min
    blk = FNET_BLOCK

    slabs0 = min(blk, n_min)
    def s0(i, c):
        r0 = pl.multiple_of(i * (slabs0 * n_maj), slabs0 * n_maj)
        v = _dot(u_ref[pl.ds(r0, slabs0 * n_maj), :], cs_ref[...])
        for s in range(slabs0):
            d0 = pl.multiple_of((i * slabs0 + s) * p1, SUBLANES)
            vr_ref[pl.ds(d0, n_maj), :] = v[s * n_maj:(s + 1) * n_maj, :LANES]
            vi_ref[pl.ds(d0, n_maj), :] = v[s * n_maj:(s + 1) * n_maj, LANES:]
        return c
    lax.fori_loop(0, n_min // slabs0, s0, 0, unroll=min(4, n_min // slabs0))

    def s1(i, c):
        cols = []
        for j in range(blk):
            n_lo = i * blk + j
            cols.append(jnp.concatenate([vr_ref[pl.ds(n_lo, n_min, stride=p1), :],
                                         vi_ref[pl.ds(n_lo, n_min, stride=p1), :]], axis=0))
        r = jnp.concatenate(cols, axis=1).astype(BF16)
        z = _dot(d1_ref[...], r)
        for j in range(blk):
            n_lo = i * blk + j
            zr = z[:n_min, j * LANES:(j + 1) * LANES]
            zi = z[n_min:, j * LANES:(j + 1) * LANES]
            tr, ti = twr_ref[n_lo], twi_ref[n_lo]
            d0 = pl.multiple_of(n_lo * p2, SUBLANES)
            zr_ref[pl.ds(d0, n_min), :] = zr * tr - zi * ti
            zi_ref[pl.ds(d0, n_min), :] = zr * ti + zi * tr
        return c
    lax.fori_loop(0, n_maj // blk, s1, 0, unroll=2)

    def s2(i, c):
        cols = []
        for j in range(blk):
            k_min = i * blk + j
            cols.append(jnp.concatenate([zr_ref[pl.ds(k_min, n_maj, stride=p2), :],
                                         zi_ref[pl.ds(k_min, n_maj, stride=p2), :]], axis=0))
        r = jnp.concatenate(cols, axis=1).astype(BF16)
        xk = _dot(d2_ref[...], r)
        for j in range(blk):
            x_ref[pl.ds(i * blk + j, n_maj, stride=p3), :] = xk[:, j * LANES:(j + 1) * LANES]
        return c
    lax.fori_loop(0, n_min // blk, s2, 0, unroll=min(2, n_min // blk))

    rows = blk * n_min
    def s3(i, c):
        four = jnp.concatenate(
            [x_ref[pl.ds(pl.multiple_of((i * blk + s) * p3, SUBLANES), n_min), :] for s in range(blk)],
            axis=0)
        mixed = _dot((four * scale).astype(BF16), wmix_ref[...]) + bmix_ref[...]
        r0 = pl.multiple_of(i * rows, rows)
        o_ref[pl.ds(r0, rows), :] = (mixed * zf_ref[pl.ds(r0, rows), :].astype(F32)).astype(o_ref.dtype)
        return c
    lax.fori_loop(0, n_maj // blk, s3, 0, unroll=4)


def _dft_mats(n):
    k = np.arange(n)
    ang = 2.0 * np.pi * ((k[:, None] * k[None, :]) % n) / n
    return np.cos(ang), np.sin(ang)


def _bf16_matrix(a):
    return jnp.asarray(a, F32).astype(BF16)


def _fnet(proj, f_mix_w, f_mix_b, *, seq, uf_blk0, zf_blk0):
    b = proj.shape[0]
    gd = LANES
    n_maj = LANES
    n_min = seq // n_maj
    p1 = n_maj + SUBLANES
    p2 = n_min + SUBLANES
    p3 = n_min + SUBLANES
    c128, s128 = _dft_mats(gd)
    cs = _bf16_matrix(np.concatenate([c128, -s128], axis=1))
    cm, sm = _dft_mats(n_min)
    d1 = _bf16_matrix(np.block([[cm, sm], [-sm, cm]]))
    d2 = _bf16_matrix(np.concatenate([c128, s128], axis=1))
    nl = np.arange(n_maj)[:, None]
    km = np.arange(n_min)[None, :]
    ang = 2.0 * np.pi * ((nl * km) % seq) / seq
    twr = jnp.broadcast_to(jnp.asarray(np.cos(ang), F32)[:, :, None], (n_maj, n_min, LANES))
    twi = jnp.broadcast_to(jnp.asarray(-np.sin(ang), F32)[:, :, None], (n_maj, n_min, LANES))
    wmix = f_mix_w.astype(BF16)
    bmix = f_mix_b.reshape(F_GROUPS, 1, gd).astype(F32)
    scale = 1.0 / math.sqrt(seq * gd)
    assert n_min % FNET_BLOCK == 0
    kern = functools.partial(_fnet_kernel, n_min=n_min, p1=p1, p2=p2, p3=p3, scale=scale)
    const2 = lambda shape: pl.BlockSpec(shape, lambda bi, g: (0, 0))
    return pl.pallas_call(
        kern,
        grid=(b, F_GROUPS),
        in_specs=[
            pl.BlockSpec((None, seq, gd), lambda bi, g: (bi, 0, uf_blk0 + g)),
            pl.BlockSpec((None, seq, gd), lambda bi, g: (bi, 0, zf_blk0 + g)),
            const2((gd, 2 * gd)),
            const2((2 * n_min, 2 * n_min)),
            pl.BlockSpec((n_maj, n_min, LANES), lambda bi, g: (0, 0, 0)),
            pl.BlockSpec((n_maj, n_min, LANES), lambda bi, g: (0, 0, 0)),
            const2((n_maj, 2 * n_maj)),
            pl.BlockSpec((None, gd, gd), lambda bi, g: (g, 0, 0)),
            pl.BlockSpec((None, 1, gd), lambda bi, g: (g, 0, 0)),
        ],
        out_specs=pl.BlockSpec((None, seq, gd), lambda bi, g: (bi, 0, g)),
        out_shape=jax.ShapeDtypeStruct((b, seq, F_GROUPS * gd), BF16),
        scratch_shapes=[
            pltpu.VMEM((n_min * p1, LANES), F32),
            pltpu.VMEM((n_min * p1, LANES), F32),
            pltpu.VMEM((n_maj * p2, LANES), F32),
            pltpu.VMEM((n_maj * p2, LANES), F32),
            pltpu.VMEM((n_maj * p3, LANES), F32),
        ],
        compiler_params=pltpu.CompilerParams(
            dimension_semantics=("arbitrary", "arbitrary"),
            vmem_limit_bytes=VMEM_LIMIT_BYTES),
        name="fnet",
    )(proj, proj, cs, d1, twr, twi, d2, wmix, bmix)


def _ssd_decays(dtt_ref, acol_ref, *, reverse):
    lc = CHUNK
    row = lax.broadcasted_iota(jnp.int32, (lc, lc), 0)
    col = lax.broadcasted_iota(jnp.int32, (lc, lc), 1)
    incl = (col >= row) if reverse else (col <= row)
    cum_t = ((row >= col) if reverse else (row <= col)).astype(BF16)

    dtt = dtt_ref[...]
    acs_t = _dot_f32_lhs(dtt * acol_ref[...], cum_t)
    end_c = acs_t[:, 0:1] if reverse else acs_t[:, lc - 1:lc]
    w_t = dtt * jnp.exp(end_c - acs_t)
    r2_t = (acs_t - jnp.log(dtt)) * LOG2E
    acs = acs_t.T
    acs2 = acs * LOG2E
    cd_row = jnp.exp(acs[0:1, :] if reverse else acs[lc - 1:lc, :])
    return incl, w_t, r2_t, acs2, cd_row


def _ssd_group_terms(b_ref, c_ref, st_ref):
    terms = []
    for g in range(N_BC_GROUPS):
        bg = b_ref[:, g * D_STATE:(g + 1) * D_STATE]
        cg = c_ref[:, g * D_STATE:(g + 1) * D_STATE]
        cb = lax.dot_general(cg, bg, (((1,), (1,)), ((), ())), preferred_element_type=F32
                             ).astype(BF16)
        bt = bg.astype(F32).T
        prev = st_ref[g]
        y_off = _dot(cg, prev.astype(BF16))
        terms.append((cb, bt, prev, y_off))
    return terms


def _ssd_direction(xs_ref, terms, decays, st_ref, y_ref, *, heads, row0):
    lc = CHUNK
    hpg = heads // N_BC_GROUPS
    gw = hpg * HEAD_DIM
    incl, w_t, r2_t, acs2, cd_row = decays

    lane = lax.broadcasted_iota(jnp.int32, (lc, LANES), 1)
    low = lane < HEAD_DIM
    low1 = low[0:1]
    zero = jnp.zeros((), BF16)

    for g in range(N_BC_GROUPS):
        cb, bt, prev, y_off = terms[g]
        for hp in range(hpg // 2):
            h0 = row0 + g * hpg + 2 * hp
            ms, bs, colbs = [], [], []
            for hr in (h0, h0 + 1):
                colb = jnp.broadcast_to(acs2[:, hr:hr + 1], (lc, lc))
                colbs.append(colb)
                decay = jnp.exp2(colb - r2_t[hr:hr + 1, :]).astype(BF16)
                ms.append(cb * jnp.where(incl, decay, zero))
                bs.append((bt * w_t[hr:hr + 1, :]).astype(BF16))
            lhs = jnp.concatenate([jnp.concatenate(ms, axis=1), jnp.concatenate(bs, axis=1)], axis=0)
            c0 = g * gw + hp * LANES
            xpair = xs_ref[:, c0:c0 + LANES]
            rhs = jnp.concatenate([jnp.where(low, xpair, zero), jnp.where(low, zero, xpair)], axis=0)
            res = _dot(lhs, rhs)
            e_pair = jnp.exp2(jnp.where(low, colbs[0], colbs[1]))
            y_ref[:, c0:c0 + LANES] = (res[:lc] + y_off[:, hp * LANES:(hp + 1) * LANES] * e_pair
                                       ).astype(y_ref.dtype)
            cd_pair = jnp.where(low1, jnp.broadcast_to(cd_row[:, h0:h0 + 1], (1, LANES)),
                                jnp.broadcast_to(cd_row[:, h0 + 1:h0 + 2], (1, LANES)))
            st_ref[g, :, hp * LANES:(hp + 1) * LANES] = (
                prev[:, hp * LANES:(hp + 1) * LANES] * cd_pair + res[lc:])


def _ssd_kernel(xsf_ref, bf_ref, cf_ref, dttf_ref, xsb_ref, bb_ref, cb_ref, dttb_ref, acol_ref,
                yf_ref, yb_ref, stf_ref, stb_ref, *, heads):
    @pl.when(pl.program_id(1) == 0)
    def _():
        stf_ref[...] = jnp.zeros_like(stf_ref)
        stb_ref[...] = jnp.zeros_like(stb_ref)

    dec_f = _ssd_decays(dttf_ref, acol_ref, reverse=False)
    dec_b = _ssd_decays(dttb_ref, acol_ref, reverse=True)
    terms_f = _ssd_group_terms(bf_ref, cf_ref, stf_ref)
    terms_b = _ssd_group_terms(bb_ref, cb_ref, stb_ref)
    _ssd_direction(xsf_ref, terms_f, dec_f, stf_ref, yf_ref, heads=heads, row0=0)
    _ssd_direction(xsb_ref, terms_b, dec_b, stb_ref, yb_ref, heads=heads, row0=heads)


def _ssd(proj3, dtt, a_col, *, seq, heads, xs_col0):
    b = proj3.shape[0]
    nc = seq // CHUNK
    width = heads * HEAD_DIM
    bc_w = N_BC_GROUPS * D_STATE
    gw = width // N_BC_GROUPS
    xs_blk = xs_col0 // width
    b_blk = (xs_col0 + width) // bc_w

    def specs(cidx):
        return [
            pl.BlockSpec((None, CHUNK, width), lambda bi, i: (bi, cidx(i), xs_blk)),
            pl.BlockSpec((None, CHUNK, bc_w), lambda bi, i: (bi, cidx(i), b_blk)),
            pl.BlockSpec((None, CHUNK, bc_w), lambda bi, i: (bi, cidx(i), b_blk + 1)),
            pl.BlockSpec((LANES, CHUNK), lambda bi, i: (0, bi * nc + cidx(i))),
        ]

    fwd = lambda i: i
    bwd = lambda i: nc - 1 - i
    kern = functools.partial(_ssd_kernel, heads=heads)
    return pl.pallas_call(
        kern,
        grid=(b, nc),
        in_specs=specs(fwd) + specs(bwd) + [pl.BlockSpec((LANES, 1), lambda bi, i: (0, 0))],
        out_specs=[
            pl.BlockSpec((None, CHUNK, width), lambda bi, i: (bi, fwd(i), 0)),
            pl.BlockSpec((None, CHUNK, width), lambda bi, i: (bi, bwd(i), 0)),
        ],
        out_shape=[jax.ShapeDtypeStruct((b, seq, width), BF16)] * 2,
        scratch_shapes=[pltpu.VMEM((N_BC_GROUPS, D_STATE, gw), F32)] * 2,
        compiler_params=pltpu.CompilerParams(
            dimension_semantics=("arbitrary", "arbitrary"),
            vmem_limit_bytes=VMEM_LIMIT_BYTES),
        name="ssd",
    )(proj3, proj3, proj3, dtt, proj3, proj3, proj3, dtt, a_col)


def _tail_kernel(yf_ref, yb_ref, xs_ref, zs_ref, gf_ref, gs_ref, hm_ref, x_ref,
                 dsk_ref, nw_ref, wbf_ref, wbs_ref, wo_ref, lng_ref, lnb_ref, o_ref, *, alpha):
    p_f = _dot(hm_ref[...], wbf_ref[...])
    y = (yf_ref[...].astype(F32) + yb_ref[...].astype(F32)
         + xs_ref[...].astype(F32) * dsk_ref[...])
    h = y * zs_ref[...].astype(F32)
    width = h.shape[1]
    gw = width // N_BC_GROUPS
    parts = []
    for g in range(N_BC_GROUPS):
        hg = h[:, g * gw:(g + 1) * gw]
        ms = jnp.mean(hg * hg, axis=-1, keepdims=True)
        parts.append(hg * lax.rsqrt(ms + NORM_EPS))
    hn = (jnp.concatenate(parts, axis=1) * nw_ref[...]).astype(BF16)
    p_s = _dot(hn, wbs_ref[...])
    merged = gf_ref[...].astype(F32) * p_f + gs_ref[...].astype(F32) * p_s
    out = _dot(merged.astype(BF16), wo_ref[...])
    r = alpha * x_ref[...] + out
    mu = jnp.mean(r, axis=-1, keepdims=True)
    rc = r - mu
    var = jnp.mean(rc * rc, axis=-1, keepdims=True)
    o_ref[...] = rc * lax.rsqrt(var + NORM_EPS) * lng_ref[...] + lnb_ref[...]


def _tail(yf, yb, proj, hmix, x2, dsk_e, nw, wbf, wbs, wo, lng, lnb, *, tm, alpha,
          xs_blk, zs_blk, gf_blk, gs_blk):
    t, d = x2.shape
    fw = hmix.shape[1]
    row = lambda shape: pl.BlockSpec(shape, lambda i: (0, 0))
    resident = lambda shape: pl.BlockSpec(shape, lambda i: (0, 0), pipeline_mode=pl.Buffered(1))
    kern = functools.partial(_tail_kernel, alpha=alpha)
    return pl.pallas_call(
        kern,
        grid=(t // tm,),
        in_specs=[
            pl.BlockSpec((tm, d), lambda i: (i, 0)),
            pl.BlockSpec((tm, d), lambda i: (i, 0)),
            pl.BlockSpec((tm, d), lambda i: (i, xs_blk)),
            pl.BlockSpec((tm, d), lambda i: (i, zs_blk)),
            pl.BlockSpec((tm, d), lambda i: (i, gf_blk)),
            pl.BlockSpec((tm, d), lambda i: (i, gs_blk)),
            pl.BlockSpec((tm, fw), lambda i: (i, 0)),
            pl.BlockSpec((tm, d), lambda i: (i, 0)),
            row((1, d)), row((1, d)),
            resident((fw, d)), resident((d, d)), resident((d, d)),
            row((1, d)), row((1, d)),
        ],
        out_specs=pl.BlockSpec((tm, d), lambda i: (i, 0)),
        out_shape=jax.ShapeDtypeStruct((t, d), F32),
        compiler_params=pltpu.CompilerParams(
            dimension_semantics=("arbitrary",),
            vmem_limit_bytes=VMEM_LIMIT_BYTES),
        name="tail",
    )(yf, yb, proj, proj, proj, proj, hmix, x2, dsk_e, nw, wbf, wbs, wo, lng, lnb)


def _pick_tile(n, target):
    t = min(n, target)
    while n % t:
        t //= 2
    return t


def _layer(x, w_in, b_gate, conv_w, conv_b, dt_bias_fwd, a_log_fwd, dt_bias_bwd, a_log_bwd,
           d_skip, ssd_norm_w, f_mix_w, f_mix_b, w_branch_f, w_branch_s, w_out, ln_g, ln_b, *, depth):
    b, seq, d = x.shape
    t = b * seq
    fw = d // 2
    heads = d // HEAD_DIM
    bc_w = N_BC_GROUPS * D_STATE
    cch = d + 2 * bc_w
    assert fw == F_GROUPS * LANES and seq % (LANES * SUBLANES) == 0 and 2 * heads <= LANES

    o_dt = 2 * fw + d + cch
    o_gate = o_dt + 2 * heads
    tn = 1024
    w_a = w_in.astype(BF16)
    w_b = w_a[:, o_gate:]
    n_b = (2 * d) // tn
    kinds = (("id",) * (fw // tn) + ("silu",) * ((fw + d) // tn) + ("sigmoid",) * n_b
             + ("conv",) * (cch // tn))
    c_gf, c_gs, c_xbc = d, 2 * d, 3 * d
    c_uf = c_xbc + cch
    c_zf = c_uf + fw
    dst_cols = ([c_uf + k * tn for k in range(fw // tn)] + [c_zf + k * tn for k in range(fw // tn)]
                + [k * tn for k in range(d // tn)] + [c_gf + k * tn for k in range(n_b)]
                + [c_xbc + k * tn for k in range(cch // tn)])
    out_blocks = tuple(c // tn for c in dst_cols)

    w_dt = jnp.concatenate([w_in[:, o_dt:o_gate], jnp.zeros((d, LANES - 2 * heads), w_in.dtype)],
                           axis=1).astype(BF16)
    b_dt = jnp.concatenate([dt_bias_fwd, dt_bias_bwd,
                            jnp.zeros((LANES - 2 * heads,), F32)]).astype(F32).reshape(1, LANES)

    x2 = x.reshape(t, d)
    proj, dtt = _inproj(x2, w_a, w_b, b_gate.astype(F32).reshape(1, 2 * d), conv_w.astype(F32),
                        conv_b.astype(F32).reshape(1, cch), w_dt, b_dt, seq=seq,
                        tm=_pick_tile(seq, 1024), tn=tn, kinds=kinds, out_blocks=out_blocks)
    nm = proj.shape[1]
    proj3 = proj.reshape(b, seq, nm)

    hmix = _fnet(proj3, f_mix_w, f_mix_b, seq=seq, uf_blk0=c_uf // LANES, zf_blk0=c_zf // LANES)

    a_col = -jnp.exp(jnp.concatenate([a_log_fwd, a_log_bwd,
                                      jnp.zeros((LANES - 2 * heads,), F32)]).astype(F32)
                     ).reshape(LANES, 1)
    yf, yb = _ssd(proj3, dtt, a_col, seq=seq, heads=heads, xs_col0=c_xbc)

    dsk_e = jnp.repeat(d_skip.astype(F32), HEAD_DIM).reshape(1, d)
    alpha = (2.0 * depth) ** 0.25
    out = _tail(yf.reshape(t, d), yb.reshape(t, d), proj, hmix.reshape(t, fw), x2,
                dsk_e, ssd_norm_w.astype(F32).reshape(1, d),
                w_branch_f.astype(BF16), w_branch_s.astype(BF16), w_out.astype(BF16),
                ln_g.astype(F32).reshape(1, d), ln_b.astype(F32).reshape(1, d),
                tm=_pick_tile(t, 256), alpha=alpha, xs_blk=c_xbc // d, zs_blk=0,
                gf_blk=c_gf // d, gs_blk=c_gs // d)
    return out.reshape(b, seq, d)


def kernel(x, w_in, b_gate, conv_w, conv_b, dt_bias_fwd, a_log_fwd, dt_bias_bwd, a_log_bwd, d_skip,
           ssd_norm_w, f_mix_w, f_mix_b, w_branch_f, w_branch_s, w_out, ln_g, ln_b):
    depth = w_in.shape[0]
    for i in range(depth):
        x = _layer(x, w_in[i], b_gate[i], conv_w[i], conv_b[i], dt_bias_fwd[i], a_log_fwd[i],
                   dt_bias_bwd[i], a_log_bwd[i], d_skip[i], ssd_norm_w[i], f_mix_w[i], f_mix_b[i],
                   w_branch_f[i], w_branch_s[i], w_out[i], ln_g[i], ln_b[i], depth=depth)
    return x
```

```python
import functools
import math

import numpy as np
import jax
import jax.numpy as jnp
from jax import lax
from jax.experimental import pallas as pl
from jax.experimental.pallas import tpu as pltpu

F32 = jnp.float32
BF16 = jnp.bfloat16

F_GROUPS = 8
HEAD_DIM = 64
N_BC_GROUPS = 4
D_STATE = 128
D_CONV = 5
CONV_PAD = D_CONV // 2
CHUNK = 128
NORM_EPS = 1e-5

LANES = 128
SUBLANES = 8
BF16_ROWS = 16
VMEM_LIMIT_BYTES = 56 * 1024 * 1024

LOG2E = 1.4426950408889634
FNET_BLOCK = 8
SSD_CHUNKS_PER_STEP = 4
INPROJ_ROWS = 256
INPROJ_CONV_ROWS = 512


def _sigmoid(v):
    return 1.0 / (1.0 + jnp.exp(-v))


def _softplus(v):
    return jnp.maximum(v, 0.0) + jnp.log1p(jnp.exp(-jnp.abs(v)))


def _dot(a, b):
    return jnp.dot(a, b, preferred_element_type=F32)


def _dot_f32_lhs(a, m):
    hi = a.astype(BF16)
    r1 = a - hi.astype(F32)
    mid = r1.astype(BF16)
    lo = (r1 - mid.astype(F32)).astype(BF16)
    return _dot(hi, m) + _dot(mid, m) + _dot(lo, m)


def _select_static(j, table):
    out = jnp.int32(table[-1])
    for idx in range(len(table) - 2, -1, -1):
        out = jnp.where(j == idx, jnp.int32(table[idx]), out)
    return out


def _repack_kernel(wt_ref, wdt_ref, o_ref, odt_ref, *, n_dt):
    o_ref[...] = wt_ref[...].T.astype(o_ref.dtype)

    @pl.when(pl.program_id(0) == 0)
    def _():
        lane = lax.broadcasted_iota(jnp.int32, odt_ref.shape, 1)
        odt_ref[...] = jnp.where(lane < n_dt, wdt_ref[...].T, 0.0).astype(odt_ref.dtype)


def _repack_w(w_t, src_rows, *, rows, dt_row0, n_dt):
    n_in, d = w_t.shape
    assert all(r % SUBLANES == 0 and r + rows <= n_in for r in src_rows)
    assert dt_row0 % SUBLANES == 0 and dt_row0 + LANES <= n_in and n_dt <= LANES
    start = lambda table: (lambda q: (_select_static(q, tuple(r // SUBLANES for r in table)) * SUBLANES, 0))
    return pl.pallas_call(
        functools.partial(_repack_kernel, n_dt=n_dt),
        grid=(len(src_rows),),
        in_specs=[
            pl.BlockSpec((pl.Element(rows), pl.Element(d)), start(src_rows)),
            pl.BlockSpec((pl.Element(LANES), pl.Element(d)), lambda q: (dt_row0, 0)),
        ],
        out_specs=[
            pl.BlockSpec((d, rows), lambda q: (0, q)),
            pl.BlockSpec((d, LANES), lambda q: (0, 0)),
        ],
        out_shape=[
            jax.ShapeDtypeStruct((d, len(src_rows) * rows), BF16),
            jax.ShapeDtypeStruct((d, LANES), BF16),
        ],
        compiler_params=pltpu.CompilerParams(
            dimension_semantics=("arbitrary",),
            vmem_limit_bytes=VMEM_LIMIT_BYTES),
        name="repack",
    )(w_t, w_t)


def _inproj_kernel(x_ref, xp_ref, xn_ref, w_ref, bg_ref, cw_ref, cb_ref, wdt_ref, bdt_ref,
                   o_ref, dtt_ref, xb_ref, cs_ref, *, blocks_per_seq, kinds):
    i = pl.program_id(0)
    j = pl.program_id(1)
    tm = x_ref.shape[0]
    halo = BF16_ROWS

    @pl.when(j == 0)
    def _():
        pos = i % blocks_per_seq
        xb_ref[pl.ds(0, halo), :] = jnp.where(pos == 0, 0.0, xp_ref[...]).astype(BF16)
        xb_ref[pl.ds(halo, tm), :] = x_ref[...].astype(BF16)
        xb_ref[pl.ds(halo + tm, halo), :] = jnp.where(pos == blocks_per_seq - 1, 0.0,
                                                      xn_ref[...]).astype(BF16)
        dt = _softplus(_dot(xb_ref[pl.ds(halo, tm), :], wdt_ref[...]) + bdt_ref[...])
        dtt_ref[...] = dt.T

    def is_kind(kind):
        m = None
        for idx, k in enumerate(kinds):
            if k == kind:
                c = j == idx
                m = c if m is None else (m | c)
        return m

    def plain(w_ref, act):
        rc = min(INPROJ_ROWS, tm)
        for r in range(tm // rc):
            acc = _dot(xb_ref[pl.ds(halo + r * rc, rc), :], w_ref[...])
            o_ref[pl.ds(r * rc, rc), :] = act(acc).astype(o_ref.dtype)

    @pl.when(is_kind("id"))
    def _():
        plain(w_ref, lambda a: a)

    @pl.when(is_kind("silu"))
    def _():
        plain(w_ref, lambda a: a * _sigmoid(a))

    @pl.when(is_kind("sigmoid"))
    def _():
        plain(w_ref, lambda a: _sigmoid(a + bg_ref[...]))

    @pl.when(is_kind("conv"))
    def _():
        rc = cs_ref.shape[2] - 2 * halo
        n_slabs = cs_ref.shape[1]
        for r in range(tm // rc):
            acc = _dot(xb_ref[pl.ds(r * rc, rc + 2 * halo), :], w_ref[...])
            for s in range(n_slabs):
                cs_ref[r, s] = acc[:, s * LANES:(s + 1) * LANES]
            for s in range(n_slabs):
                lanes = pl.ds(s * LANES, LANES)
                out = cb_ref[:, lanes]
                for k in range(D_CONV):
                    out = out + (cs_ref[r, s, pl.ds(halo - CONV_PAD + k, rc), :]
                                 * cw_ref[pl.ds(k, 1), lanes])
                o_ref[pl.ds(r * rc, rc), lanes] = (out * _sigmoid(out)).astype(o_ref.dtype)


def _inproj(x2, w_main, b_gate, conv_w, conv_b, w_dt, b_dt, *, seq, tm, tn, kinds, out_blocks):
    t, d = x2.shape
    n_tiles = len(kinds)
    n_b = kinds.count("sigmoid")
    n_conv = kinds.count("conv")
    n_pre = n_tiles - n_b - n_conv
    assert kinds[n_pre:n_pre + n_b] == ("sigmoid",) * n_b and kinds[n_pre + n_b:] == ("conv",) * n_conv
    hb = tm // BF16_ROWS
    last_hb = t // BF16_ROWS - 1
    conv_rows = min(INPROJ_CONV_ROWS, tm)
    gate_blk = lambda j: jnp.where(j < n_pre, n_b - 1, jnp.minimum(j - n_pre, n_b - 1))
    conv_blk = lambda j: jnp.where(j < n_pre + n_b, n_conv - 1, j - n_pre - n_b)
    kern = functools.partial(_inproj_kernel, blocks_per_seq=seq // tm, kinds=kinds)
    return pl.pallas_call(
        kern,
        grid=(t // tm, n_tiles),
        in_specs=[
            pl.BlockSpec((tm, d), lambda i, j: (i, 0)),
            pl.BlockSpec((BF16_ROWS, d), lambda i, j: (jnp.maximum(i * hb - 1, 0), 0)),
            pl.BlockSpec((BF16_ROWS, d), lambda i, j: (jnp.minimum((i + 1) * hb, last_hb), 0)),
            pl.BlockSpec((d, tn), lambda i, j: (0, j)),
            pl.BlockSpec((1, tn), lambda i, j: (0, gate_blk(j))),
            pl.BlockSpec((D_CONV, tn), lambda i, j: (0, conv_blk(j))),
            pl.BlockSpec((1, tn), lambda i, j: (0, conv_blk(j))),
            pl.BlockSpec((d, LANES), lambda i, j: (0, 0)),
            pl.BlockSpec((1, LANES), lambda i, j: (0, 0)),
        ],
        out_specs=[
            pl.BlockSpec((tm, tn), lambda i, j: (i, _select_static(j, out_blocks))),
            pl.BlockSpec((LANES, tm), lambda i, j: (0, i)),
        ],
        out_shape=[
            jax.ShapeDtypeStruct((t, n_tiles * tn), BF16),
            jax.ShapeDtypeStruct((LANES, t), F32),
        ],
        scratch_shapes=[
            pltpu.VMEM((tm + 2 * BF16_ROWS, d), BF16),
            pltpu.VMEM((tm // conv_rows, tn // LANES, conv_rows + 2 * BF16_ROWS, LANES), F32),
        ],
        compiler_params=pltpu.CompilerParams(
            dimension_semantics=("arbitrary", "arbitrary"),
            vmem_limit_bytes=VMEM_LIMIT_BYTES),
        name="inproj",
    )(x2, x2, x2, w_main, b_gate, conv_w, conv_b, w_dt, b_dt)


def _fnet_kernel(u_ref, zf_ref, cs_ref, d1_ref, twr_ref, twi_ref, d2_ref, wmix_ref, bmix_ref,
                 o_ref, vr_ref, vi_ref, zr_ref, zi_ref, x_ref, *, n_min, p1, p2, p3):
    seq = u_ref.shape[0]
    n_maj = seq // n_min
    blk = FNET_BLOCK

    slabs0 = min(blk, n_min)
    def s0(i, c):
        r0 = pl.multiple_of(i * (slabs0 * n_maj), slabs0 * n_maj)
        v = _dot(u_ref[pl.ds(r0, slabs0 * n_maj), :], cs_ref[...])
        for s in range(slabs0):
            d0 = pl.multiple_of((i * slabs0 + s) * p1, SUBLANES)
            vr_ref[pl.ds(d0, n_maj), :] = v[s * n_maj:(s + 1) * n_maj, :LANES]
            vi_ref[pl.ds(d0, n_maj), :] = v[s * n_maj:(s + 1) * n_maj, LANES:]
        return c
    lax.fori_loop(0, n_min // slabs0, s0, 0, unroll=min(4, n_min // slabs0))

    def s1(i, c):
        cols = []
        for j in range(blk):
            n_lo = i * blk + j
            cols.append(jnp.concatenate([vr_ref[pl.ds(n_lo, n_min, stride=p1), :],
                                         vi_ref[pl.ds(n_lo, n_min, stride=p1), :]], axis=0))
        r = jnp.concatenate(cols, axis=1).astype(BF16)
        z = _dot(d1_ref[...], r)
        for j in range(blk):
            n_lo = i * blk + j
            zr = z[:n_min, j * LANES:(j + 1) * LANES]
            zi = z[n_min:, j * LANES:(j + 1) * LANES]
            tr, ti = twr_ref[n_lo], twi_ref[n_lo]
            d0 = pl.multiple_of(n_lo * p2, SUBLANES)
            zr_ref[pl.ds(d0, n_min), :] = zr * tr - zi * ti
            zi_ref[pl.ds(d0, n_min), :] = zr * ti + zi * tr
        return c
    lax.fori_loop(0, n_maj // blk, s1, 0, unroll=2)

    def s2(i, c):
        cols = []
        for j in range(blk):
            k_min = i * blk + j
            cols.append(jnp.concatenate([zr_ref[pl.ds(k_min, n_maj, stride=p2), :],
                                         zi_ref[pl.ds(k_min, n_maj, stride=p2), :]], axis=0))
        r = jnp.concatenate(cols, axis=1).astype(BF16)
        xk = _dot(d2_ref[...], r)
        for j in range(blk):
            x_ref[pl.ds(i * blk + j, n_maj, stride=p3), :] = xk[:, j * LANES:(j + 1) * LANES]
        return c
    lax.fori_loop(0, n_min // blk, s2, 0, unroll=min(2, n_min // blk))

    rows = blk * n_min
    def s3(i, c):
        four = jnp.concatenate(
            [x_ref[pl.ds(pl.multiple_of((i * blk + s) * p3, SUBLANES), n_min), :] for s in range(blk)],
            axis=0)
        mixed = _dot(four.astype(BF16), wmix_ref[...]) + bmix_ref[...]
        r0 = pl.multiple_of(i * rows, rows)
        o_ref[pl.ds(r0, rows), :] = (mixed * zf_ref[pl.ds(r0, rows), :].astype(F32)).astype(o_ref.dtype)
        return c
    lax.fori_loop(0, n_maj // blk, s3, 0, unroll=4)


def _dft_mats(n):
    k = np.arange(n)
    ang = 2.0 * np.pi * ((k[:, None] * k[None, :]) % n) / n
    return np.cos(ang), np.sin(ang)


def _bf16_matrix(a):
    return jnp.asarray(a, F32).astype(BF16)


def _fnet(proj, f_mix_w, f_mix_b, *, seq, uf_blk0, zf_blk0):
    b = proj.shape[0]
    gd = LANES
    n_maj = LANES
    n_min = seq // n_maj
    p1 = n_maj + SUBLANES
    p2 = n_min + SUBLANES
    p3 = n_min + SUBLANES
    c128, s128 = _dft_mats(gd)
    cs = _bf16_matrix(np.concatenate([c128, -s128], axis=1))
    cm, sm = _dft_mats(n_min)
    d1 = _bf16_matrix(np.block([[cm, sm], [-sm, cm]]))
    scale = 1.0 / math.sqrt(seq * gd)
    d2 = _bf16_matrix(np.concatenate([c128, s128], axis=1) * scale)
    nl = np.arange(n_maj)[:, None]
    km = np.arange(n_min)[None, :]
    ang = 2.0 * np.pi * ((nl * km) % seq) / seq
    twr = jnp.broadcast_to(jnp.asarray(np.cos(ang), F32)[:, :, None], (n_maj, n_min, LANES))
    twi = jnp.broadcast_to(jnp.asarray(-np.sin(ang), F32)[:, :, None], (n_maj, n_min, LANES))
    wmix = f_mix_w.astype(BF16)
    bmix = f_mix_b.reshape(F_GROUPS, 1, gd).astype(F32)
    assert n_min % FNET_BLOCK == 0
    kern = functools.partial(_fnet_kernel, n_min=n_min, p1=p1, p2=p2, p3=p3)
    const2 = lambda shape: pl.BlockSpec(shape, lambda bi, g: (0, 0))
    return pl.pallas_call(
        kern,
        grid=(b, F_GROUPS),
        in_specs=[
            pl.BlockSpec((None, seq, gd), lambda bi, g: (bi, 0, uf_blk0 + g)),
            pl.BlockSpec((None, seq, gd), lambda bi, g: (bi, 0, zf_blk0 + g)),
            const2((gd, 2 * gd)),
            const2((2 * n_min, 2 * n_min)),
            pl.BlockSpec((n_maj, n_min, LANES), lambda bi, g: (0, 0, 0)),
            pl.BlockSpec((n_maj, n_min, LANES), lambda bi, g: (0, 0, 0)),
            const2((n_maj, 2 * n_maj)),
            pl.BlockSpec((None, gd, gd), lambda bi, g: (g, 0, 0)),
            pl.BlockSpec((None, 1, gd), lambda bi, g: (g, 0, 0)),
        ],
        out_specs=pl.BlockSpec((None, seq, gd), lambda bi, g: (bi, 0, g)),
        out_shape=jax.ShapeDtypeStruct((b, seq, F_GROUPS * gd), BF16),
        scratch_shapes=[
            pltpu.VMEM((n_min * p1, LANES), F32),
            pltpu.VMEM((n_min * p1, LANES), F32),
            pltpu.VMEM((n_maj * p2, LANES), F32),
            pltpu.VMEM((n_maj * p2, LANES), F32),
            pltpu.VMEM((n_maj * p3, LANES), F32),
        ],
        compiler_params=pltpu.CompilerParams(
            dimension_semantics=("arbitrary", "arbitrary"),
            vmem_limit_bytes=VMEM_LIMIT_BYTES),
        name="fnet",
    )(proj, proj, cs, d1, twr, twi, d2, wmix, bmix)


def _ssd_decays(dtt_ref, acol_ref, *, reverse):
    lc = CHUNK
    row = lax.broadcasted_iota(jnp.int32, (lc, lc), 0)
    col = lax.broadcasted_iota(jnp.int32, (lc, lc), 1)
    incl = (col >= row) if reverse else (col <= row)
    cum_t = ((row >= col) if reverse else (row <= col)).astype(BF16)

    dtt = dtt_ref[...]
    acs_t = _dot_f32_lhs(dtt * acol_ref[...], cum_t)
    end_c = acs_t[:, 0:1] if reverse else acs_t[:, lc - 1:lc]
    w_t = dtt * jnp.exp(end_c - acs_t)
    r2_t = (acs_t - jnp.log(dtt)) * LOG2E
    acs = acs_t.T
    acs2 = acs * LOG2E
    cd_row = jnp.exp(acs[0:1, :] if reverse else acs[lc - 1:lc, :])
    return incl, w_t, r2_t, acs2, cd_row


def _ssd_group_terms(b_ref, c_ref, st_ref):
    terms = []
    for g in range(N_BC_GROUPS):
        bg = b_ref[:, g * D_STATE:(g + 1) * D_STATE]
        cg = c_ref[:, g * D_STATE:(g + 1) * D_STATE]
        cb = lax.dot_general(cg, bg, (((1,), (1,)), ((), ())), preferred_element_type=F32
                             ).astype(BF16)
        bt = bg.astype(F32).T
        prev = st_ref[g]
        y_off = _dot(cg, prev.astype(BF16))
        terms.append((cb, bt, prev, y_off))
    return terms


def _ssd_direction(xs_ref, terms, decays, st_ref, y_ref, *, heads, row0):
    lc = CHUNK
    hpg = heads // N_BC_GROUPS
    gw = hpg * HEAD_DIM
    incl, w_t, r2_t, acs2, cd_row = decays

    lane = lax.broadcasted_iota(jnp.int32, (lc, LANES), 1)
    low = lane < HEAD_DIM
    low1 = low[0:1]
    zero = jnp.zeros((), BF16)

    for g in range(N_BC_GROUPS):
        cb, bt, prev, y_off = terms[g]
        for hp in range(hpg // 2):
            h0 = row0 + g * hpg + 2 * hp
            ms, bs, colbs = [], [], []
            for hr in (h0, h0 + 1):
                colb = jnp.broadcast_to(acs2[:, hr:hr + 1], (lc, lc))
                colbs.append(colb)
                decay = jnp.exp2(colb - r2_t[hr:hr + 1, :]).astype(BF16)
                ms.append(cb * jnp.where(incl, decay, zero))
                bs.append((bt * w_t[hr:hr + 1, :]).astype(BF16))
            lhs = jnp.concatenate([jnp.concatenate(ms, axis=1), jnp.concatenate(bs, axis=1)], axis=0)
            c0 = g * gw + hp * LANES
            xpair = xs_ref[:, c0:c0 + LANES]
            rhs = jnp.concatenate([jnp.where(low, xpair, zero), jnp.where(low, zero, xpair)], axis=0)
            res = _dot(lhs, rhs)
            e_pair = jnp.exp2(jnp.where(low, colbs[0], colbs[1]))
            y_ref[:, c0:c0 + LANES] = (res[:lc] + y_off[:, hp * LANES:(hp + 1) * LANES] * e_pair
                                       ).astype(y_ref.dtype)
            cd_pair = jnp.where(low1, jnp.broadcast_to(cd_row[:, h0:h0 + 1], (1, LANES)),
                                jnp.broadcast_to(cd_row[:, h0 + 1:h0 + 2], (1, LANES)))
            st_ref[g, :, hp * LANES:(hp + 1) * LANES] = (
                prev[:, hp * LANES:(hp + 1) * LANES] * cd_pair + res[lc:])


def _ssd_kernel(xsf_ref, bf_ref, cf_ref, dttf_ref, xsb_ref, bb_ref, cb_ref, dttb_ref, acol_ref,
                yf_ref, yb_ref, stf_ref, stb_ref, *, heads):
    @pl.when(pl.program_id(1) == 0)
    def _():
        stf_ref[...] = jnp.zeros_like(stf_ref)
        stb_ref[...] = jnp.zeros_like(stb_ref)

    n_sub = xsf_ref.shape[0] // CHUNK
    rows = lambda ref, c: ref.at[pl.ds(c * CHUNK, CHUNK), :]
    cols = lambda ref, c: ref.at[:, pl.ds(c * CHUNK, CHUNK)]
    dec_f = [_ssd_decays(cols(dttf_ref, c), acol_ref, reverse=False) for c in range(n_sub)]
    dec_b = [_ssd_decays(cols(dttb_ref, c), acol_ref, reverse=True) for c in range(n_sub)]
    for k in range(n_sub):
        cf, cb_ = k, n_sub - 1 - k
        terms_f = _ssd_group_terms(rows(bf_ref, cf), rows(cf_ref, cf), stf_ref)
        terms_b = _ssd_group_terms(rows(bb_ref, cb_), rows(cb_ref, cb_), stb_ref)
        _ssd_direction(rows(xsf_ref, cf), terms_f, dec_f[cf], stf_ref, rows(yf_ref, cf), heads=heads, row0=0)
        _ssd_direction(rows(xsb_ref, cb_), terms_b, dec_b[cb_], stb_ref, rows(yb_ref, cb_), heads=heads,
                       row0=heads)


def _ssd(proj3, dtt, a_col, *, seq, heads, xs_col0):
    b = proj3.shape[0]
    blk = math.gcd(SSD_CHUNKS_PER_STEP * CHUNK, seq)
    nc = seq // blk
    width = heads * HEAD_DIM
    bc_w = N_BC_GROUPS * D_STATE
    gw = width // N_BC_GROUPS
    xs_blk = xs_col0 // width
    b_blk = (xs_col0 + width) // bc_w

    def specs(cidx):
        return [
            pl.BlockSpec((None, blk, width), lambda bi, i: (bi, cidx(i), xs_blk)),
            pl.BlockSpec((None, blk, bc_w), lambda bi, i: (bi, cidx(i), b_blk)),
            pl.BlockSpec((None, blk, bc_w), lambda bi, i: (bi, cidx(i), b_blk + 1)),
            pl.BlockSpec((LANES, blk), lambda bi, i: (0, bi * nc + cidx(i))),
        ]

    fwd = lambda i: i
    bwd = lambda i: nc - 1 - i
    kern = functools.partial(_ssd_kernel, heads=heads)
    return pl.pallas_call(
        kern,
        grid=(b, nc),
        in_specs=specs(fwd) + specs(bwd) + [pl.BlockSpec((LANES, 1), lambda bi, i: (0, 0))],
        out_specs=[
            pl.BlockSpec((None, blk, width), lambda bi, i: (bi, fwd(i), 0)),
            pl.BlockSpec((None, blk, width), lambda bi, i: (bi, bwd(i), 0)),
        ],
        out_shape=[jax.ShapeDtypeStruct((b, seq, width), BF16)] * 2,
        scratch_shapes=[pltpu.VMEM((N_BC_GROUPS, D_STATE, gw), F32)] * 2,
        compiler_params=pltpu.CompilerParams(
            dimension_semantics=("arbitrary", "arbitrary"),
            vmem_limit_bytes=VMEM_LIMIT_BYTES),
        name="ssd",
    )(proj3, proj3, proj3, dtt, proj3, proj3, proj3, dtt, a_col)


def _tail_kernel(yf_ref, yb_ref, xs_ref, zs_ref, gf_ref, gs_ref, hm_ref, x_ref,
                 dsk_ref, nw_ref, wbf_ref, wbs_ref, wo_ref, lng_ref, lnb_ref, o_ref, *, alpha):
    p_f = _dot(hm_ref[...], wbf_ref[...])
    y = (yf_ref[...].astype(F32) + yb_ref[...].astype(F32)
         + xs_ref[...].astype(F32) * dsk_ref[...])
    h = y * zs_ref[...].astype(F32)
    width = h.shape[1]
    gw = width // N_BC_GROUPS
    parts = []
    for g in range(N_BC_GROUPS):
        hg = h[:, g * gw:(g + 1) * gw]
        ms = jnp.mean(hg * hg, axis=-1, keepdims=True)
        parts.append(hg * lax.rsqrt(ms + NORM_EPS))
    hn = (jnp.concatenate(parts, axis=1) * nw_ref[...]).astype(BF16)
    p_s = _dot(hn, wbs_ref[...])
    merged = gf_ref[...].astype(F32) * p_f + gs_ref[...].astype(F32) * p_s
    out = _dot(merged.astype(BF16), wo_ref[...])
    r = alpha * x_ref[...] + out
    mu = jnp.mean(r, axis=-1, keepdims=True)
    rc = r - mu
    var = jnp.mean(rc * rc, axis=-1, keepdims=True)
    o_ref[...] = rc * lax.rsqrt(var + NORM_EPS) * lng_ref[...] + lnb_ref[...]


def _tail(yf, yb, proj, hmix, x2, dsk_e, nw, wbf, wbs, wo, lng, lnb, *, tm, alpha,
          xs_blk, zs_blk, gf_blk, gs_blk):
    t, d = x2.shape
    fw = hmix.shape[1]
    row = lambda shape: pl.BlockSpec(shape, lambda i: (0, 0))
    resident = lambda shape: pl.BlockSpec(shape, lambda i: (0, 0), pipeline_mode=pl.Buffered(1))
    kern = functools.partial(_tail_kernel, alpha=alpha)
    return pl.pallas_call(
        kern,
        grid=(t // tm,),
        in_specs=[
            pl.BlockSpec((tm, d), lambda i: (i, 0)),
            pl.BlockSpec((tm, d), lambda i: (i, 0)),
            pl.BlockSpec((tm, d), lambda i: (i, xs_blk)),
            pl.BlockSpec((tm, d), lambda i: (i, zs_blk)),
            pl.BlockSpec((tm, d), lambda i: (i, gf_blk)),
            pl.BlockSpec((tm, d), lambda i: (i, gs_blk)),
            pl.BlockSpec((tm, fw), lambda i: (i, 0)),
            pl.BlockSpec((tm, d), lambda i: (i, 0)),
            row((1, d)), row((1, d)),
            resident((fw, d)), resident((d, d)), resident((d, d)),
            row((1, d)), row((1, d)),
        ],
        out_specs=pl.BlockSpec((tm, d), lambda i: (i, 0)),
        out_shape=jax.ShapeDtypeStruct((t, d), F32),
        compiler_params=pltpu.CompilerParams(
            dimension_semantics=("arbitrary",),
            vmem_limit_bytes=VMEM_LIMIT_BYTES),
        name="tail",
    )(yf, yb, proj, proj, proj, proj, hmix, x2, dsk_e, nw, wbf, wbs, wo, lng, lnb)


def _pick_tile(n, target):
    t = min(n, target)
    while n % t:
        t //= 2
    return t


def _layer(x, w_in, b_gate, conv_w, conv_b, dt_bias_fwd, a_log_fwd, dt_bias_bwd, a_log_bwd,
           d_skip, ssd_norm_w, f_mix_w, f_mix_b, w_branch_f, w_branch_s, w_out, ln_g, ln_b, *, depth):
    b, seq, d = x.shape
    t = b * seq
    fw = d // 2
    heads = d // HEAD_DIM
    bc_w = N_BC_GROUPS * D_STATE
    cch = d + 2 * bc_w
    assert fw == F_GROUPS * LANES and seq % (LANES * SUBLANES) == 0 and 2 * heads <= LANES

    o_dt = 2 * fw + d + cch
    o_gate = o_dt + 2 * heads
    tn = 1024
    w_t = w_in.T
    n_b = (2 * d) // tn
    kinds = (("id",) * (fw // tn) + ("silu",) * ((fw + d) // tn) + ("sigmoid",) * n_b
             + ("conv",) * (cch // tn))
    c_gf, c_gs, c_xbc = d, 2 * d, 3 * d
    c_uf = c_xbc + cch
    c_zf = c_uf + fw
    dst_cols = ([c_uf + k * tn for k in range(fw // tn)] + [c_zf + k * tn for k in range(fw // tn)]
                + [k * tn for k in range(d // tn)] + [c_gf + k * tn for k in range(n_b)]
                + [c_xbc + k * tn for k in range(cch // tn)])
    out_blocks = tuple(c // tn for c in dst_cols)
    tiles = lambda start, width: [start + k * tn for k in range(width // tn)]
    src_rows = tuple(tiles(0, fw) + tiles(fw, fw) + tiles(2 * fw, d) + tiles(o_gate, 2 * d)
                     + tiles(2 * fw + d, cch))
    w_main, w_dt = _repack_w(w_t, src_rows, rows=tn, dt_row0=o_dt, n_dt=2 * heads)

    b_dt = jnp.concatenate([dt_bias_fwd, dt_bias_bwd,
                            jnp.zeros((LANES - 2 * heads,), F32)]).astype(F32).reshape(1, LANES)

    x2 = x.reshape(t, d)
    proj, dtt = _inproj(x2, w_main, b_gate.astype(F32).reshape(1, 2 * d), conv_w.astype(F32),
                        conv_b.astype(F32).reshape(1, cch), w_dt, b_dt, seq=seq,
                        tm=_pick_tile(seq, 1024), tn=tn, kinds=kinds, out_blocks=out_blocks)
    nm = proj.shape[1]
    proj3 = proj.reshape(b, seq, nm)

    hmix = _fnet(proj3, f_mix_w, f_mix_b, seq=seq, uf_blk0=c_uf // LANES, zf_blk0=c_zf // LANES)

    a_col = -jnp.exp(jnp.concatenate([a_log_fwd, a_log_bwd,
                                      jnp.zeros((LANES - 2 * heads,), F32)]).astype(F32)
                     ).reshape(LANES, 1)
    yf, yb = _ssd(proj3, dtt, a_col, seq=seq, heads=heads, xs_col0=c_xbc)

    dsk_e = jnp.repeat(d_skip.astype(F32), HEAD_DIM).reshape(1, d)
    alpha = (2.0 * depth) ** 0.25
    out = _tail(yf.reshape(t, d), yb.reshape(t, d), proj, hmix.reshape(t, fw), x2,
                dsk_e, ssd_norm_w.astype(F32).reshape(1, d),
                w_branch_f.astype(BF16), w_branch_s.astype(BF16), w_out.astype(BF16),
                ln_g.astype(F32).reshape(1, d), ln_b.astype(F32).reshape(1, d),
                tm=_pick_tile(t, 256), alpha=alpha, xs_blk=c_xbc // d, zs_blk=0,
                gf_blk=c_gf // d, gs_blk=c_gs // d)
    return out.reshape(b, seq, d)


def kernel(x, w_in, b_gate, conv_w, conv_b, dt_bias_fwd, a_log_fwd, dt_bias_bwd, a_log_bwd, d_skip,
           ssd_norm_w, f_mix_w, f_mix_b, w_branch_f, w_branch_s, w_out, ln_g, ln_b):
    depth = w_in.shape[0]
    for i in range(depth):
        x = _layer(x, w_in[i], b_gate[i], conv_w[i], conv_b[i], dt_bias_fwd[i], a_log_fwd[i],
                   dt_bias_bwd[i], a_log_bwd[i], d_skip[i], ssd_norm_w[i], f_mix_w[i], f_mix_b[i],
                   w_branch_f[i], w_branch_s[i], w_out[i], ln_g[i], ln_b[i], depth=depth)
    return x
```

```python
import functools
import math

import numpy as np
import jax
import jax.numpy as jnp
from jax import lax
from jax.experimental import pallas as pl
from jax.experimental.pallas import tpu as pltpu

F32 = jnp.float32
BF16 = jnp.bfloat16

F_GROUPS = 8
HEAD_DIM = 64
N_BC_GROUPS = 4
D_STATE = 128
D_CONV = 5
CONV_PAD = D_CONV // 2
CHUNK = 128
NORM_EPS = 1e-5

LANES = 128
SUBLANES = 8
BF16_ROWS = 16
VMEM_LIMIT_BYTES = 56 * 1024 * 1024

LOG2E = 1.4426950408889634
FNET_BLOCK = 16
SSD_CHUNKS_PER_STEP = 8
INPROJ_ROWS = 256
INPROJ_CONV_ROWS = 512


def _sigmoid(v):
    return 1.0 / (1.0 + jnp.exp(-v))


def _softplus(v):
    return jnp.maximum(v, 0.0) + jnp.log1p(jnp.exp(-jnp.abs(v)))


def _dot(a, b):
    return jnp.dot(a, b, preferred_element_type=F32)


def _dot_f32_lhs(a, m):
    hi = a.astype(BF16)
    r1 = a - hi.astype(F32)
    mid = r1.astype(BF16)
    lo = (r1 - mid.astype(F32)).astype(BF16)
    return _dot(hi, m) + _dot(mid, m) + _dot(lo, m)


def _select_static(j, table):
    out = jnp.int32(table[-1])
    for idx in range(len(table) - 2, -1, -1):
        out = jnp.where(j == idx, jnp.int32(table[idx]), out)
    return out


def _repack_kernel(wt_ref, wdt_ref, o_ref, odt_ref, *, n_dt):
    o_ref[...] = wt_ref[...].T.astype(o_ref.dtype)

    @pl.when(pl.program_id(0) == 0)
    def _():
        lane = lax.broadcasted_iota(jnp.int32, odt_ref.shape, 1)
        odt_ref[...] = jnp.where(lane < n_dt, wdt_ref[...].T, 0.0).astype(odt_ref.dtype)


def _repack_w(w_t, src_rows, *, rows, dt_row0, n_dt):
    n_in, d = w_t.shape
    assert all(r % SUBLANES == 0 and r + rows <= n_in for r in src_rows)
    assert dt_row0 % SUBLANES == 0 and dt_row0 + LANES <= n_in and n_dt <= LANES
    start = lambda table: (lambda q: (_select_static(q, tuple(r // SUBLANES for r in table)) * SUBLANES, 0))
    return pl.pallas_call(
        functools.partial(_repack_kernel, n_dt=n_dt),
        grid=(len(src_rows),),
        in_specs=[
            pl.BlockSpec((pl.Element(rows), pl.Element(d)), start(src_rows)),
            pl.BlockSpec((pl.Element(LANES), pl.Element(d)), lambda q: (dt_row0, 0)),
        ],
        out_specs=[
            pl.BlockSpec((d, rows), lambda q: (0, q)),
            pl.BlockSpec((d, LANES), lambda q: (0, 0)),
        ],
        out_shape=[
            jax.ShapeDtypeStruct((d, len(src_rows) * rows), BF16),
            jax.ShapeDtypeStruct((d, LANES), BF16),
        ],
        compiler_params=pltpu.CompilerParams(
            dimension_semantics=("arbitrary",),
            vmem_limit_bytes=VMEM_LIMIT_BYTES),
        name="repack",
    )(w_t, w_t)


def _inproj_kernel(x_ref, xp_ref, xn_ref, w_ref, bg_ref, cw_ref, cb_ref, wdt_ref, bdt_ref,
                   o_ref, dtt_ref, xb_ref, cs_ref, *, blocks_per_seq, kinds):
    i = pl.program_id(0)
    j = pl.program_id(1)
    tm = x_ref.shape[0]
    halo = BF16_ROWS

    @pl.when(j == 0)
    def _():
        pos = i % blocks_per_seq
        xb_ref[pl.ds(0, halo), :] = jnp.where(pos == 0, 0.0, xp_ref[...]).astype(BF16)
        xb_ref[pl.ds(halo, tm), :] = x_ref[...].astype(BF16)
        xb_ref[pl.ds(halo + tm, halo), :] = jnp.where(pos == blocks_per_seq - 1, 0.0,
                                                      xn_ref[...]).astype(BF16)
        dt = _softplus(_dot(xb_ref[pl.ds(halo, tm), :], wdt_ref[...]) + bdt_ref[...])
        dtt_ref[...] = dt.T

    def is_kind(kind):
        m = None
        for idx, k in enumerate(kinds):
            if k == kind:
                c = j == idx
                m = c if m is None else (m | c)
        return m

    def plain(w_ref, act):
        rc = min(INPROJ_ROWS, tm)
        for r in range(tm // rc):
            acc = _dot(xb_ref[pl.ds(halo + r * rc, rc), :], w_ref[...])
            o_ref[pl.ds(r * rc, rc), :] = act(acc).astype(o_ref.dtype)

    @pl.when(is_kind("id"))
    def _():
        plain(w_ref, lambda a: a)

    @pl.when(is_kind("silu"))
    def _():
        plain(w_ref, lambda a: a * _sigmoid(a))

    @pl.when(is_kind("sigmoid"))
    def _():
        plain(w_ref, lambda a: _sigmoid(a + bg_ref[...]))

    @pl.when(is_kind("conv"))
    def _():
        rc = cs_ref.shape[2] - 2 * halo
        n_slabs = cs_ref.shape[1]
        for r in range(tm // rc):
            acc = _dot(xb_ref[pl.ds(r * rc, rc + 2 * halo), :], w_ref[...])
            for s in range(n_slabs):
                cs_ref[r, s] = acc[:, s * LANES:(s + 1) * LANES]
            for s in range(n_slabs):
                lanes = pl.ds(s * LANES, LANES)
                out = cb_ref[:, lanes]
                for k in range(D_CONV):
                    out = out + (cs_ref[r, s, pl.ds(halo - CONV_PAD + k, rc), :]
                                 * cw_ref[pl.ds(k, 1), lanes])
                o_ref[pl.ds(r * rc, rc), lanes] = (out * _sigmoid(out)).astype(o_ref.dtype)


def _inproj(x2, w_main, b_gate, conv_w, conv_b, w_dt, b_dt, *, seq, tm, tn, kinds, out_blocks):
    t, d = x2.shape
    n_tiles = len(kinds)
    n_b = kinds.count("sigmoid")
    n_conv = kinds.count("conv")
    n_pre = n_tiles - n_b - n_conv
    assert kinds[n_pre:n_pre + n_b] == ("sigmoid",) * n_b and kinds[n_pre + n_b:] == ("conv",) * n_conv
    hb = tm // BF16_ROWS
    last_hb = t // BF16_ROWS - 1
    conv_rows = min(INPROJ_CONV_ROWS, tm)
    gate_blk = lambda j: jnp.where(j < n_pre, n_b - 1, jnp.minimum(j - n_pre, n_b - 1))
    conv_blk = lambda j: jnp.where(j < n_pre + n_b, n_conv - 1, j - n_pre - n_b)
    kern = functools.partial(_inproj_kernel, blocks_per_seq=seq // tm, kinds=kinds)
    return pl.pallas_call(
        kern,
        grid=(t // tm, n_tiles),
        in_specs=[
            pl.BlockSpec((tm, d), lambda i, j: (i, 0)),
            pl.BlockSpec((BF16_ROWS, d), lambda i, j: (jnp.maximum(i * hb - 1, 0), 0)),
            pl.BlockSpec((BF16_ROWS, d), lambda i, j: (jnp.minimum((i + 1) * hb, last_hb), 0)),
            pl.BlockSpec((d, tn), lambda i, j: (0, j)),
            pl.BlockSpec((1, tn), lambda i, j: (0, gate_blk(j))),
            pl.BlockSpec((D_CONV, tn), lambda i, j: (0, conv_blk(j))),
            pl.BlockSpec((1, tn), lambda i, j: (0, conv_blk(j))),
            pl.BlockSpec((d, LANES), lambda i, j: (0, 0)),
            pl.BlockSpec((1, LANES), lambda i, j: (0, 0)),
        ],
        out_specs=[
            pl.BlockSpec((tm, tn), lambda i, j: (i, _select_static(j, out_blocks))),
            pl.BlockSpec((LANES, tm), lambda i, j: (0, i)),
        ],
        out_shape=[
            jax.ShapeDtypeStruct((t, n_tiles * tn), BF16),
            jax.ShapeDtypeStruct((LANES, t), F32),
        ],
        scratch_shapes=[
            pltpu.VMEM((tm + 2 * BF16_ROWS, d), BF16),
            pltpu.VMEM((tm // conv_rows, tn // LANES, conv_rows + 2 * BF16_ROWS, LANES), F32),
        ],
        compiler_params=pltpu.CompilerParams(
            dimension_semantics=("arbitrary", "arbitrary"),
            vmem_limit_bytes=VMEM_LIMIT_BYTES),
        name="inproj",
    )(x2, x2, x2, w_main, b_gate, conv_w, conv_b, w_dt, b_dt)


def _fnet_kernel(u_ref, zf_ref, cs_ref, d1_ref, twr_ref, twi_ref, d2_ref, wmix_ref, bmix_ref,
                 o_ref, vr_ref, vi_ref, zr_ref, zi_ref, x_ref, *, n_min, p1, p2, p3):
    seq = u_ref.shape[0]
    n_maj = seq // n_min
    blk = math.gcd(FNET_BLOCK, n_min)

    slabs0 = min(blk, n_min)
    def s0(i, c):
        r0 = pl.multiple_of(i * (slabs0 * n_maj), slabs0 * n_maj)
        v = _dot(u_ref[pl.ds(r0, slabs0 * n_maj), :], cs_ref[...])
        for s in range(slabs0):
            d0 = pl.multiple_of((i * slabs0 + s) * p1, SUBLANES)
            vr_ref[pl.ds(d0, n_maj), :] = v[s * n_maj:(s + 1) * n_maj, :LANES]
            vi_ref[pl.ds(d0, n_maj), :] = v[s * n_maj:(s + 1) * n_maj, LANES:]
        return c
    lax.fori_loop(0, n_min // slabs0, s0, 0, unroll=min(4, n_min // slabs0))

    def s1(i, c):
        cols = []
        for j in range(blk):
            n_lo = i * blk + j
            cols.append(jnp.concatenate([vr_ref[pl.ds(n_lo, n_min, stride=p1), :],
                                         vi_ref[pl.ds(n_lo, n_min, stride=p1), :]], axis=0))
        r = jnp.concatenate(cols, axis=1).astype(BF16)
        z = _dot(d1_ref[...], r)
        for j in range(blk):
            n_lo = i * blk + j
            zr = z[:n_min, j * LANES:(j + 1) * LANES]
            zi = z[n_min:, j * LANES:(j + 1) * LANES]
            tr, ti = twr_ref[n_lo], twi_ref[n_lo]
            d0 = pl.multiple_of(n_lo * p2, SUBLANES)
            zr_ref[pl.ds(d0, n_min), :] = zr * tr - zi * ti
            zi_ref[pl.ds(d0, n_min), :] = zr * ti + zi * tr
        return c
    lax.fori_loop(0, n_maj // blk, s1, 0, unroll=2)

    def s2(i, c):
        cols = []
        for j in range(blk):
            k_min = i * blk + j
            cols.append(jnp.concatenate([zr_ref[pl.ds(k_min, n_maj, stride=p2), :],
                                         zi_ref[pl.ds(k_min, n_maj, stride=p2), :]], axis=0))
        r = jnp.concatenate(cols, axis=1).astype(BF16)
        xk = _dot(d2_ref[...], r)
        for j in range(blk):
            x_ref[pl.ds(i * blk + j, n_maj, stride=p3), :] = xk[:, j * LANES:(j + 1) * LANES]
        return c
    lax.fori_loop(0, n_min // blk, s2, 0, unroll=min(2, n_min // blk))

    rows = blk * n_min
    def s3(i, c):
        four = jnp.concatenate(
            [x_ref[pl.ds(pl.multiple_of((i * blk + s) * p3, SUBLANES), n_min), :] for s in range(blk)],
            axis=0)
        mixed = _dot(four.astype(BF16), wmix_ref[...]) + bmix_ref[...]
        r0 = pl.multiple_of(i * rows, rows)
        o_ref[pl.ds(r0, rows), :] = (mixed * zf_ref[pl.ds(r0, rows), :].astype(F32)).astype(o_ref.dtype)
        return c
    lax.fori_loop(0, n_maj // blk, s3, 0, unroll=4)


def _dft_mats(n):
    k = np.arange(n)
    ang = 2.0 * np.pi * ((k[:, None] * k[None, :]) % n) / n
    return np.cos(ang), np.sin(ang)


def _bf16_matrix(a):
    return jnp.asarray(a, F32).astype(BF16)


def _fnet(proj, f_mix_w, f_mix_b, *, seq, uf_blk0, zf_blk0):
    b = proj.shape[0]
    gd = LANES
    n_maj = LANES
    n_min = seq // n_maj
    p1 = n_maj + SUBLANES
    p2 = n_min + SUBLANES
    p3 = n_min + SUBLANES
    c128, s128 = _dft_mats(gd)
    cs = _bf16_matrix(np.concatenate([c128, -s128], axis=1))
    cm, sm = _dft_mats(n_min)
    d1 = _bf16_matrix(np.block([[cm, sm], [-sm, cm]]))
    scale = 1.0 / math.sqrt(seq * gd)
    d2 = _bf16_matrix(np.concatenate([c128, s128], axis=1) * scale)
    nl = np.arange(n_maj)[:, None]
    km = np.arange(n_min)[None, :]
    ang = 2.0 * np.pi * ((nl * km) % seq) / seq
    twr = jnp.broadcast_to(jnp.asarray(np.cos(ang), F32)[:, :, None], (n_maj, n_min, LANES))
    twi = jnp.broadcast_to(jnp.asarray(-np.sin(ang), F32)[:, :, None], (n_maj, n_min, LANES))
    wmix = f_mix_w.astype(BF16)
    bmix = f_mix_b.reshape(F_GROUPS, 1, gd).astype(F32)
    kern = functools.partial(_fnet_kernel, n_min=n_min, p1=p1, p2=p2, p3=p3)
    const2 = lambda shape: pl.BlockSpec(shape, lambda bi, g: (0, 0))
    return pl.pallas_call(
        kern,
        grid=(b, F_GROUPS),
        in_specs=[
            pl.BlockSpec((None, seq, gd), lambda bi, g: (bi, 0, uf_blk0 + g)),
            pl.BlockSpec((None, seq, gd), lambda bi, g: (bi, 0, zf_blk0 + g)),
            const2((gd, 2 * gd)),
            const2((2 * n_min, 2 * n_min)),
            pl.BlockSpec((n_maj, n_min, LANES), lambda bi, g: (0, 0, 0)),
            pl.BlockSpec((n_maj, n_min, LANES), lambda bi, g: (0, 0, 0)),
            const2((n_maj, 2 * n_maj)),
            pl.BlockSpec((None, gd, gd), lambda bi, g: (g, 0, 0)),
            pl.BlockSpec((None, 1, gd), lambda bi, g: (g, 0, 0)),
        ],
        out_specs=pl.BlockSpec((None, seq, gd), lambda bi, g: (bi, 0, g)),
        out_shape=jax.ShapeDtypeStruct((b, seq, F_GROUPS * gd), BF16),
        scratch_shapes=[
            pltpu.VMEM((n_min * p1, LANES), F32),
            pltpu.VMEM((n_min * p1, LANES), F32),
            pltpu.VMEM((n_maj * p2, LANES), F32),
            pltpu.VMEM((n_maj * p2, LANES), F32),
            pltpu.VMEM((n_maj * p3, LANES), F32),
        ],
        compiler_params=pltpu.CompilerParams(
            dimension_semantics=("arbitrary", "arbitrary"),
            vmem_limit_bytes=VMEM_LIMIT_BYTES),
        name="fnet",
    )(proj, proj, cs, d1, twr, twi, d2, wmix, bmix)


def _ssd_decays(dtt_ref, acol_ref, *, reverse):
    lc = CHUNK
    row = lax.broadcasted_iota(jnp.int32, (lc, lc), 0)
    col = lax.broadcasted_iota(jnp.int32, (lc, lc), 1)
    incl = (col >= row) if reverse else (col <= row)
    cum_t = ((row >= col) if reverse else (row <= col)).astype(BF16)

    dtt = dtt_ref[...]
    acs_t = _dot_f32_lhs(dtt * acol_ref[...], cum_t)
    end_c = acs_t[:, 0:1] if reverse else acs_t[:, lc - 1:lc]
    w_t = dtt * jnp.exp(end_c - acs_t)
    r2_t = (acs_t - jnp.log(dtt)) * LOG2E
    acs = acs_t.T
    acs2 = acs * LOG2E
    cd_row = jnp.exp(acs[0:1, :] if reverse else acs[lc - 1:lc, :])
    return incl, w_t, r2_t, acs2, cd_row


def _ssd_group_terms(b_ref, c_ref, st_ref):
    terms = []
    for g in range(N_BC_GROUPS):
        bg = b_ref[:, g * D_STATE:(g + 1) * D_STATE]
        cg = c_ref[:, g * D_STATE:(g + 1) * D_STATE]
        cb = lax.dot_general(cg, bg, (((1,), (1,)), ((), ())), preferred_element_type=F32
                             ).astype(BF16)
        bt = bg.astype(F32).T
        prev = st_ref[g]
        y_off = _dot(cg, prev.astype(BF16))
        terms.append((cb, bt, prev, y_off))
    return terms


def _ssd_direction(xs_ref, terms, decays, st_ref, y_ref, *, heads, row0):
    lc = CHUNK
    hpg = heads // N_BC_GROUPS
    gw = hpg * HEAD_DIM
    incl, w_t, r2_t, acs2, cd_row = decays

    lane = lax.broadcasted_iota(jnp.int32, (lc, LANES), 1)
    low = lane < HEAD_DIM
    low1 = low[0:1]
    zero = jnp.zeros((), BF16)

    for g in range(N_BC_GROUPS):
        cb, bt, prev, y_off = terms[g]
        for hp in range(hpg // 2):
            h0 = row0 + g * hpg + 2 * hp
            ms, bs, colbs = [], [], []
            for hr in (h0, h0 + 1):
                colb = jnp.broadcast_to(acs2[:, hr:hr + 1], (lc, lc))
                colbs.append(colb)
                decay = jnp.exp2(colb - r2_t[hr:hr + 1, :]).astype(BF16)
                ms.append(cb * jnp.where(incl, decay, zero))
                bs.append((bt * w_t[hr:hr + 1, :]).astype(BF16))
            lhs = jnp.concatenate([jnp.concatenate(ms, axis=1), jnp.concatenate(bs, axis=1)], axis=0)
            c0 = g * gw + hp * LANES
            xpair = xs_ref[:, c0:c0 + LANES]
            rhs = jnp.concatenate([jnp.where(low, xpair, zero), jnp.where(low, zero, xpair)], axis=0)
            res = _dot(lhs, rhs)
            e_pair = jnp.exp2(jnp.where(low, colbs[0], colbs[1]))
            y_ref[:, c0:c0 + LANES] = (res[:lc] + y_off[:, hp * LANES:(hp + 1) * LANES] * e_pair
                                       ).astype(y_ref.dtype)
            cd_pair = jnp.where(low1, jnp.broadcast_to(cd_row[:, h0:h0 + 1], (1, LANES)),
                                jnp.broadcast_to(cd_row[:, h0 + 1:h0 + 2], (1, LANES)))
            st_ref[g, :, hp * LANES:(hp + 1) * LANES] = (
                prev[:, hp * LANES:(hp + 1) * LANES] * cd_pair + res[lc:])


def _ssd_kernel(xsf_ref, bf_ref, cf_ref, dttf_ref, xsb_ref, bb_ref, cb_ref, dttb_ref, acol_ref,
                yf_ref, yb_ref, stf_ref, stb_ref, *, heads):
    @pl.when(pl.program_id(1) == 0)
    def _():
        stf_ref[...] = jnp.zeros_like(stf_ref)
        stb_ref[...] = jnp.zeros_like(stb_ref)

    n_sub = xsf_ref.shape[0] // CHUNK
    rows = lambda ref, c: ref.at[pl.ds(c * CHUNK, CHUNK), :]
    cols = lambda ref, c: ref.at[:, pl.ds(c * CHUNK, CHUNK)]
    dec_f = [_ssd_decays(cols(dttf_ref, c), acol_ref, reverse=False) for c in range(n_sub)]
    dec_b = [_ssd_decays(cols(dttb_ref, c), acol_ref, reverse=True) for c in range(n_sub)]
    for k in range(n_sub):
        cf, cb_ = k, n_sub - 1 - k
        terms_f = _ssd_group_terms(rows(bf_ref, cf), rows(cf_ref, cf), stf_ref)
        terms_b = _ssd_group_terms(rows(bb_ref, cb_), rows(cb_ref, cb_), stb_ref)
        _ssd_direction(rows(xsf_ref, cf), terms_f, dec_f[cf], stf_ref, rows(yf_ref, cf), heads=heads, row0=0)
        _ssd_direction(rows(xsb_ref, cb_), terms_b, dec_b[cb_], stb_ref, rows(yb_ref, cb_), heads=heads,
                       row0=heads)


def _ssd(proj3, dtt, a_col, *, seq, heads, xs_col0):
    b = proj3.shape[0]
    blk = math.gcd(SSD_CHUNKS_PER_STEP * CHUNK, seq)
    nc = seq // blk
    width = heads * HEAD_DIM
    bc_w = N_BC_GROUPS * D_STATE
    gw = width // N_BC_GROUPS
    xs_blk = xs_col0 // width
    b_blk = (xs_col0 + width) // bc_w

    def specs(cidx):
        return [
            pl.BlockSpec((None, blk, width), lambda bi, i: (bi, cidx(i), xs_blk)),
            pl.BlockSpec((None, blk, bc_w), lambda bi, i: (bi, cidx(i), b_blk)),
            pl.BlockSpec((None, blk, bc_w), lambda bi, i: (bi, cidx(i), b_blk + 1)),
            pl.BlockSpec((LANES, blk), lambda bi, i: (0, bi * nc + cidx(i))),
        ]

    fwd = lambda i: i
    bwd = lambda i: nc - 1 - i
    kern = functools.partial(_ssd_kernel, heads=heads)
    return pl.pallas_call(
        kern,
        grid=(b, nc),
        in_specs=specs(fwd) + specs(bwd) + [pl.BlockSpec((LANES, 1), lambda bi, i: (0, 0))],
        out_specs=[
            pl.BlockSpec((None, blk, width), lambda bi, i: (bi, fwd(i), 0)),
            pl.BlockSpec((None, blk, width), lambda bi, i: (bi, bwd(i), 0)),
        ],
        out_shape=[jax.ShapeDtypeStruct((b, seq, width), BF16)] * 2,
        scratch_shapes=[pltpu.VMEM((N_BC_GROUPS, D_STATE, gw), F32)] * 2,
        compiler_params=pltpu.CompilerParams(
            dimension_semantics=("arbitrary", "arbitrary"),
            vmem_limit_bytes=VMEM_LIMIT_BYTES),
        name="ssd",
    )(proj3, proj3, proj3, dtt, proj3, proj3, proj3, dtt, a_col)


def _tail_kernel(yf_ref, yb_ref, xs_ref, zs_ref, gf_ref, gs_ref, hm_ref, x_ref,
                 dsk_ref, nw_ref, wbf_ref, wbs_ref, wo_ref, lng_ref, lnb_ref, o_ref, *, alpha):
    p_f = _dot(hm_ref[...], wbf_ref[...])
    y = (yf_ref[...].astype(F32) + yb_ref[...].astype(F32)
         + xs_ref[...].astype(F32) * dsk_ref[...])
    h = y * zs_ref[...].astype(F32)
    width = h.shape[1]
    gw = width // N_BC_GROUPS
    parts = []
    for g in range(N_BC_GROUPS):
        hg = h[:, g * gw:(g + 1) * gw]
        ms = jnp.mean(hg * hg, axis=-1, keepdims=True)
        parts.append(hg * lax.rsqrt(ms + NORM_EPS))
    hn = (jnp.concatenate(parts, axis=1) * nw_ref[...]).astype(BF16)
    p_s = _dot(hn, wbs_ref[...])
    merged = gf_ref[...].astype(F32) * p_f + gs_ref[...].astype(F32) * p_s
    out = _dot(merged.astype(BF16), wo_ref[...])
    r = alpha * x_ref[...] + out
    mu = jnp.mean(r, axis=-1, keepdims=True)
    rc = r - mu
    var = jnp.mean(rc * rc, axis=-1, keepdims=True)
    o_ref[...] = rc * lax.rsqrt(var + NORM_EPS) * lng_ref[...] + lnb_ref[...]


def _tail(yf, yb, proj, hmix, x2, dsk_e, nw, wbf, wbs, wo, lng, lnb, *, tm, alpha,
          xs_blk, zs_blk, gf_blk, gs_blk):
    t, d = x2.shape
    fw = hmix.shape[1]
    row = lambda shape: pl.BlockSpec(shape, lambda i: (0, 0))
    resident = lambda shape: pl.BlockSpec(shape, lambda i: (0, 0), pipeline_mode=pl.Buffered(1))
    kern = functools.partial(_tail_kernel, alpha=alpha)
    return pl.pallas_call(
        kern,
        grid=(t // tm,),
        in_specs=[
            pl.BlockSpec((tm, d), lambda i: (i, 0)),
            pl.BlockSpec((tm, d), lambda i: (i, 0)),
            pl.BlockSpec((tm, d), lambda i: (i, xs_blk)),
            pl.BlockSpec((tm, d), lambda i: (i, zs_blk)),
            pl.BlockSpec((tm, d), lambda i: (i, gf_blk)),
            pl.BlockSpec((tm, d), lambda i: (i, gs_blk)),
            pl.BlockSpec((tm, fw), lambda i: (i, 0)),
            pl.BlockSpec((tm, d), lambda i: (i, 0)),
            row((1, d)), row((1, d)),
            resident((fw, d)), resident((d, d)), resident((d, d)),
            row((1, d)), row((1, d)),
        ],
        out_specs=pl.BlockSpec((tm, d), lambda i: (i, 0)),
        out_shape=jax.ShapeDtypeStruct((t, d), F32),
        compiler_params=pltpu.CompilerParams(
            dimension_semantics=("arbitrary",),
            vmem_limit_bytes=VMEM_LIMIT_BYTES),
        name="tail",
    )(yf, yb, proj, proj, proj, proj, hmix, x2, dsk_e, nw, wbf, wbs, wo, lng, lnb)


def _pick_tile(n, target):
    t = min(n, target)
    while n % t:
        t //= 2
    return t


def _layer(x, w_in, b_gate, conv_w, conv_b, dt_bias_fwd, a_log_fwd, dt_bias_bwd, a_log_bwd,
           d_skip, ssd_norm_w, f_mix_w, f_mix_b, w_branch_f, w_branch_s, w_out, ln_g, ln_b, *, depth):
    b, seq, d = x.shape
    t = b * seq
    fw = d // 2
    heads = d // HEAD_DIM
    bc_w = N_BC_GROUPS * D_STATE
    cch = d + 2 * bc_w
    assert fw == F_GROUPS * LANES and seq % (LANES * SUBLANES) == 0 and 2 * heads <= LANES

    o_dt = 2 * fw + d + cch
    o_gate = o_dt + 2 * heads
    tn = 1024
    w_t = w_in.T
    n_b = (2 * d) // tn
    kinds = (("id",) * (fw // tn) + ("silu",) * ((fw + d) // tn) + ("sigmoid",) * n_b
             + ("conv",) * (cch // tn))
    c_gf, c_gs, c_xbc = d, 2 * d, 3 * d
    c_uf = c_xbc + cch
    c_zf = c_uf + fw
    dst_cols = ([c_uf + k * tn for k in range(fw // tn)] + [c_zf + k * tn for k in range(fw // tn)]
                + [k * tn for k in range(d // tn)] + [c_gf + k * tn for k in range(n_b)]
                + [c_xbc + k * tn for k in range(cch // tn)])
    out_blocks = tuple(c // tn for c in dst_cols)
    tiles = lambda start, width: [start + k * tn for k in range(width // tn)]
    src_rows = tuple(tiles(0, fw) + tiles(fw, fw) + tiles(2 * fw, d) + tiles(o_gate, 2 * d)
                     + tiles(2 * fw + d, cch))
    w_main, w_dt = _repack_w(w_t, src_rows, rows=tn, dt_row0=o_dt, n_dt=2 * heads)

    b_dt = jnp.concatenate([dt_bias_fwd, dt_bias_bwd,
                            jnp.zeros((LANES - 2 * heads,), F32)]).astype(F32).reshape(1, LANES)

    x2 = x.reshape(t, d)
    proj, dtt = _inproj(x2, w_main, b_gate.astype(F32).reshape(1, 2 * d), conv_w.astype(F32),
                        conv_b.astype(F32).reshape(1, cch), w_dt, b_dt, seq=seq,
                        tm=_pick_tile(seq, 1024), tn=tn, kinds=kinds, out_blocks=out_blocks)
    nm = proj.shape[1]
    proj3 = proj.reshape(b, seq, nm)

    hmix = _fnet(proj3, f_mix_w, f_mix_b, seq=seq, uf_blk0=c_uf // LANES, zf_blk0=c_zf // LANES)

    a_col = -jnp.exp(jnp.concatenate([a_log_fwd, a_log_bwd,
                                      jnp.zeros((LANES - 2 * heads,), F32)]).astype(F32)
                     ).reshape(LANES, 1)
    yf, yb = _ssd(proj3, dtt, a_col, seq=seq, heads=heads, xs_col0=c_xbc)

    dsk_e = jnp.repeat(d_skip.astype(F32), HEAD_DIM).reshape(1, d)
    alpha = (2.0 * depth) ** 0.25
    out = _tail(yf.reshape(t, d), yb.reshape(t, d), proj, hmix.reshape(t, fw), x2,
                dsk_e, ssd_norm_w.astype(F32).reshape(1, d),
                w_branch_f.astype(BF16), w_branch_s.astype(BF16), w_out.astype(BF16),
                ln_g.astype(F32).reshape(1, d), ln_b.astype(F32).reshape(1, d),
                tm=_pick_tile(t, 256), alpha=alpha, xs_blk=c_xbc // d, zs_blk=0,
                gf_blk=c_gf // d, gs_blk=c_gs // d)
    return out.reshape(b, seq, d)


def kernel(x, w_in, b_gate, conv_w, conv_b, dt_bias_fwd, a_log_fwd, dt_bias_bwd, a_log_bwd, d_skip,
           ssd_norm_w, f_mix_w, f_mix_b, w_branch_f, w_branch_s, w_out, ln_g, ln_b):
    depth = w_in.shape[0]
    for i in range(depth):
        x = _layer(x, w_in[i], b_gate[i], conv_w[i], conv_b[i], dt_bias_fwd[i], a_log_fwd[i],
                   dt_bias_bwd[i], a_log_bwd[i], d_skip[i], ssd_norm_w[i], f_mix_w[i], f_mix_b[i],
                   w_branch_f[i], w_branch_s[i], w_out[i], ln_g[i], ln_b[i], depth=depth)
    return x
```
